```python
import math
import jax, jax.numpy as jnp
from jax import lax
import numpy as np

D_MODEL = 2048
BATCH = 2
SEQ = 4096
DEPTH = 4

N_MIXERS = 2
CONV_WIDTH = 3
HEAD_DIM = 128
N_HEADS = D_MODEL // HEAD_DIM
N_KV_HEADS = 4
Q_PER_KV = N_HEADS // N_KV_HEADS
QKV_DIM = (N_HEADS + 2 * N_KV_HEADS) * HEAD_DIM
WINDOW = 128
ATTN_BLOCK = 128
ROPE_THETA = 10000.0
D_FF = (7 * D_MODEL) // 2
N_EXPERTS = 8
TOP_K = 2
RMS_EPS = 1e-6
MASK_VALUE = -1e30
N_CONV_LAYERS = (DEPTH + 1) // 2
N_ATTN_LAYERS = DEPTH // 2
N_DENSE_LAYERS = (DEPTH + 1) // 2
N_MOE_LAYERS = DEPTH // 2

kernel_name = "hybrid_conv_swa_moe_encoder"


def rms_norm(x, g):
    xf = x.astype(jnp.float32)
    y = xf * lax.rsqrt(jnp.mean(xf * xf, axis=-1, keepdims=True) + RMS_EPS)
    return (y * g.astype(jnp.float32)).astype(x.dtype)


def apply_rope(x, positions):
    half = HEAD_DIM // 2
    inv_freq = ROPE_THETA ** (-jnp.arange(0, half, dtype=jnp.float32) / half)
    ang = positions.astype(jnp.float32)[:, :, None] * inv_freq[None, None, :]
    cos = jnp.cos(ang)[:, :, None, :]
    sin = jnp.sin(ang)[:, :, None, :]
    xf = x.astype(jnp.float32)
    x1, x2 = xf[..., :half], xf[..., half:]
    out = jnp.concatenate([x1 * cos - x2 * sin, x2 * cos + x1 * sin], axis=-1)
    return out.astype(x.dtype)


def short_conv_mixer(h, w_in, w_conv, w_out):
    proj = jnp.einsum('bsd,de->bse', h, w_in)
    gate_b, gate_c, val = jnp.split(proj, 3, axis=-1)
    u = gate_c * val
    up = jnp.pad(u, ((0, 0), (1, 1), (0, 0)))
    conv = w_conv[0] * up[:, :-2] + w_conv[1] * up[:, 1:-1] + w_conv[2] * up[:, 2:]
    return jnp.einsum('bse,ed->bsd', gate_b * conv, w_out)


def window_attention(h, positions, w_qkv, q_gain, k_gain, sink, w_out):
    b, s, _ = h.shape
    nb = s // ATTN_BLOCK
    qkv = jnp.einsum('bsd,de->bse', h, w_qkv)
    q = qkv[..., :N_HEADS * HEAD_DIM].reshape(b, s, N_HEADS, HEAD_DIM)
    k = qkv[..., N_HEADS * HEAD_DIM:(N_HEADS + N_KV_HEADS) * HEAD_DIM].reshape(b, s, N_KV_HEADS, HEAD_DIM)
    v = qkv[..., (N_HEADS + N_KV_HEADS) * HEAD_DIM:].reshape(b, s, N_KV_HEADS, HEAD_DIM)
    q = apply_rope(rms_norm(q, q_gain), positions)
    k = apply_rope(rms_norm(k, k_gain), positions)
    q = q.reshape(b, nb, ATTN_BLOCK, N_KV_HEADS, Q_PER_KV, HEAD_DIM)

    def band(t):
        tp = jnp.pad(t, ((0, 0), (ATTN_BLOCK, ATTN_BLOCK), (0, 0), (0, 0)))
        tp = tp.reshape(b, nb + 2, ATTN_BLOCK, N_KV_HEADS, HEAD_DIM)
        return jnp.concatenate([tp[:, :-2], tp[:, 1:-1], tp[:, 2:]], axis=2)

    kb, vb = band(k), band(v)
    n_idx = jnp.arange(nb)[:, None, None]
    q_idx = n_idx * ATTN_BLOCK + jnp.arange(ATTN_BLOCK)[None, :, None]
    k_idx = (n_idx - 1) * ATTN_BLOCK + jnp.arange(3 * ATTN_BLOCK)[None, None, :]
    mask = (jnp.abs(k_idx - q_idx) <= WINDOW) & (k_idx >= 0) & (k_idx < s)

    scores = jnp.einsum('bnqhgd,bnkhd->bnhgqk', q, kb,
                        preferred_element_type=jnp.float32) * (1.0 / math.sqrt(HEAD_DIM))
    scores = jnp.where(mask[None, :, None, None], scores, MASK_VALUE)
    sink_l = sink.astype(jnp.float32).reshape(1, 1, N_KV_HEADS, Q_PER_KV, 1, 1)
    m = jnp.maximum(jnp.max(scores, axis=-1, keepdims=True), sink_l)
    p = jnp.exp(scores - m)
    denom = jnp.sum(p, axis=-1, keepdims=True) + jnp.exp(sink_l - m)
    probs = (p / denom).astype(vb.dtype)
    o = jnp.einsum('bnhgqk,bnkhd->bnqhgd', probs, vb).reshape(b, s, N_HEADS * HEAD_DIM)
    return jnp.einsum('bse,ed->bsd', o, w_out)


def swiglu(h, w13, w2):
    gu = jnp.einsum('...d,df->...f', h, w13)
    g, u = jnp.split(gu, 2, axis=-1)
    return jnp.einsum('...f,fd->...d', jax.nn.silu(g) * u, w2)


def moe_swiglu(h, w_router, w13, w2):
    b, s, d = h.shape
    hf = h.reshape(b * s, d)
    logits = jnp.einsum('td,de->te', hf, w_router, preferred_element_type=jnp.float32)
    top_v, top_i = lax.top_k(logits, TOP_K)
    gates = jax.nn.softmax(top_v, axis=-1)
    combine = jnp.sum(jax.nn.one_hot(top_i, N_EXPERTS, dtype=jnp.float32)
                      * gates[..., None], axis=1).astype(h.dtype)
    out = jnp.zeros_like(hf)
    for e in range(N_EXPERTS):
        out = out + combine[:, e:e + 1] * swiglu(hf, w13[e], w2[e])
    return out.reshape(b, s, d)


def setup_inputs(seed: int = 0) -> dict:
    key = jax.random.key(seed)
    ks = jax.random.split(key, 20)
    D = D_MODEL

    def nrm(k, shape, scale):
        return jax.random.normal(k, shape, jnp.float32) * scale

    x = jax.random.normal(ks[0], (BATCH, SEQ, D), jnp.float32)
    offset = jax.random.randint(ks[1], (BATCH, 1), 0, 4096, dtype=jnp.int32)
    positions = offset + jnp.arange(SEQ, dtype=jnp.int32)[None, :]
    return {
        "x": x,
        "positions": positions,
        "norm_mix": 1.0 + nrm(ks[2], (DEPTH, D), 0.02),
        "norm_ffn": 1.0 + nrm(ks[3], (DEPTH, D), 0.02),
        "conv_in": nrm(ks[4], (N_CONV_LAYERS, D, 3 * D), D ** -0.5),
        "conv_w": nrm(ks[5], (N_CONV_LAYERS, CONV_WIDTH, D), CONV_WIDTH ** -0.5),
        "conv_out": nrm(ks[6], (N_CONV_LAYERS, D, D), D ** -0.5),
        "attn_qkv": nrm(ks[7], (N_ATTN_LAYERS, D, QKV_DIM), D ** -0.5),
        "q_norm": 1.0 + nrm(ks[8], (N_ATTN_LAYERS, HEAD_DIM), 0.02),
        "k_norm": 1.0 + nrm(ks[9], (N_ATTN_LAYERS, HEAD_DIM), 0.02),
        "attn_sink": nrm(ks[10], (N_ATTN_LAYERS, N_HEADS), 1.0),
        "attn_out": nrm(ks[11], (N_ATTN_LAYERS, N_HEADS * HEAD_DIM, D), (N_HEADS * HEAD_DIM) ** -0.5),
        "ffn_w13": nrm(ks[12], (N_DENSE_LAYERS, D, 2 * D_FF), D ** -0.5),
        "ffn_w2": nrm(ks[13], (N_DENSE_LAYERS, D_FF, D), D_FF ** -0.5),
        "router": nrm(ks[14], (N_MOE_LAYERS, D, N_EXPERTS), D ** -0.5),
        "moe_w13": nrm(ks[15], (N_MOE_LAYERS, N_EXPERTS, D, 2 * D_FF), D ** -0.5),
        "moe_w2": nrm(ks[16], (N_MOE_LAYERS, N_EXPERTS, D_FF, D), D_FF ** -0.5),
    }


def reference(x, positions, norm_mix, norm_ffn, conv_in, conv_w, conv_out,
              attn_qkv, q_norm, k_norm, attn_sink, attn_out,
              ffn_w13, ffn_w2, router, moe_w13, moe_w2):
    for i in range(DEPTH):
        j = i // 2
        h = rms_norm(x, norm_mix[i])
        if i % N_MIXERS == 0:
            x = x + short_conv_mixer(h, conv_in[j], conv_w[j], conv_out[j])
        else:
            x = x + window_attention(h, positions, attn_qkv[j], q_norm[j], k_norm[j],
                                     attn_sink[j], attn_out[j])
        h = rms_norm(x, norm_ffn[i])
        if i % 2 == 0:
            x = x + swiglu(h, ffn_w13[j], ffn_w2[j])
        else:
            x = x + moe_swiglu(h, router[j], moe_w13[j], moe_w2[j])
    return x
```

```python
import functools
import math

import jax
import jax.numpy as jnp
from jax import lax
from jax.experimental import pallas as pl
from jax.experimental.pallas import tpu as pltpu

F32 = jnp.float32
BF16 = jnp.bfloat16
I32 = jnp.int32

HEAD_DIM = 128
N_KV_HEADS = 4
Q_PER_KV = 4
WINDOW = 128
ATTN_BLOCK = 128
ROPE_THETA = 10000.0
N_EXPERTS = 8
RMS_EPS = 1e-6
MASK_VALUE = -1e30

LANES = 128
BF16_SUBLANES = 16
VMEM_BUDGET = 60000 * 1024

MOE_TM = 512


def _params(vmem_bytes, n_grid):
    return pltpu.CompilerParams(
        dimension_semantics=("arbitrary",) * n_grid,
        vmem_limit_bytes=int(min(vmem_bytes, VMEM_BUDGET)))


def _nbytes(shape, dtype):
    return math.prod(shape) * jnp.dtype(dtype).itemsize


def _rms(x, g):
    ms = jnp.mean(x * x, axis=-1, keepdims=True)
    return x * lax.rsqrt(ms + RMS_EPS) * g


def _norm_kernel(x_ref, g_ref, o_ref):
    o_ref[...] = _rms(x_ref[...], g_ref[...]).astype(o_ref.dtype)


def rms_norm_bf16(x, g, tm=512):
    t, d = x.shape
    return pl.pallas_call(
        _norm_kernel,
        grid=(t // tm,),
        in_specs=[pl.BlockSpec((tm, d), lambda i: (i, 0)),
                  pl.BlockSpec((1, d), lambda i: (0, 0))],
        out_specs=pl.BlockSpec((tm, d), lambda i: (i, 0)),
        out_shape=jax.ShapeDtypeStruct((t, d), BF16),
        compiler_params=_params(6 * tm * d * 4, 1),
        name="rms_norm",
    )(x, g.reshape(1, d))


def _gmm_kernel(te_ref, tv_ref, x_ref, *refs, n_w, n_extra, n_out, epilogue):
    w_refs = refs[:n_w]
    e_refs = refs[n_w:n_w + n_extra]
    o_refs = refs[n_w + n_extra:n_w + n_extra + n_out]
    wb_refs = refs[n_w + n_extra + n_out:]
    j = pl.program_id(0)
    i = pl.program_id(1)
    cur = te_ref[i]
    prev = te_ref[jnp.maximum(i - 1, 0)]

    @pl.when(jnp.logical_or(i == 0, cur != prev))
    def _cast_weights():
        for w, wb in zip(w_refs, wb_refs):
            wb[...] = w[...].astype(BF16)

    @pl.when(tv_ref[i] != 0)
    def _compute():
        xb = x_ref[...]
        accs = [jnp.dot(xb, wb[...], preferred_element_type=F32) for wb in wb_refs]
        epilogue(j, accs, e_refs, o_refs)

    @pl.when(tv_ref[i] == 0)
    def _skip():
        for o in o_refs:
            o[...] = jnp.zeros(o.shape, o.dtype)


def grouped_matmul(x, weights, te, tv, *, tm, tn, n_col_tiles, extras, out_dtypes,
                   out_cols, epilogue, name):
    rows, k = x.shape
    n_w, n_extra, n_out = len(weights), len(extras), len(out_dtypes)

    in_specs = [pl.BlockSpec((tm, k), lambda j, i, te, tv: (i, 0))]
    for _, off in weights:
        in_specs.append(pl.BlockSpec(
            (None, k, tn), lambda j, i, te, tv, off=off: (te[i], 0, j + off)))
    for _, bshape, imap in extras:
        in_specs.append(pl.BlockSpec(bshape, lambda j, i, te, tv, imap=imap: imap(j, i)))
    out_specs = [pl.BlockSpec((tm, tn), lambda j, i, te, tv: (i, j)) for _ in out_dtypes]
    out_shape = [jax.ShapeDtypeStruct((rows, out_cols), dt) for dt in out_dtypes]

    vmem = 2 * _nbytes((tm, k), x.dtype)
    vmem += n_w * (2 * _nbytes((k, tn), F32) + _nbytes((k, tn), BF16))
    vmem += sum(2 * _nbytes(b, a.dtype) for a, b, _ in extras)
    vmem += sum(2 * _nbytes((tm, tn), dt) for dt in out_dtypes)
    vmem += (n_w + 2) * _nbytes((tm, tn), F32)
    vmem += 4 * 1024 * 1024

    outs = pl.pallas_call(
        functools.partial(_gmm_kernel, n_w=n_w, n_extra=n_extra, n_out=n_out,
                          epilogue=epilogue),
        grid_spec=pltpu.PrefetchScalarGridSpec(
            num_scalar_prefetch=2,
            grid=(n_col_tiles, rows // tm),
            in_specs=in_specs,
            out_specs=out_specs,
            scratch_shapes=[pltpu.VMEM((k, tn), BF16) for _ in weights]),
        out_shape=out_shape,
        compiler_params=_params(vmem, 2),
        name=name,
    )(te, tv, x, *[w for w, _ in weights], *[a for a, _, _ in extras])
    return outs


def _dense_tiles(rows, tm):
    n = rows // tm
    return jnp.zeros((n,), I32), jnp.ones((n,), I32)


def _epi_swiglu(j, accs, e_refs, o_refs):
    g, u = accs
    o_refs[0][...] = (g * jax.nn.sigmoid(g) * u).astype(BF16)


def _epi_residual(j, accs, e_refs, o_refs):
    o_refs[0][...] = e_refs[0][...] + accs[0]


def _epi_plain(j, accs, e_refs, o_refs):
    o_refs[0][...] = accs[0]


def _epi_conv_in(j, accs, e_refs, o_refs):
    gate_b, gate_c, val = accs
    o_refs[0][...] = gate_b.astype(BF16)
    o_refs[1][...] = (gate_c * val).astype(BF16)


def _epi_qkv(j, accs, e_refs, o_refs, *, heads_per_tile, n_rope_tiles):
    acc = accs[0]
    cos_ref, sin_ref, gain_ref = e_refs
    o = o_refs[0]

    @pl.when(j < n_rope_tiles)
    def _qk():
        cos = cos_ref[...]
        sin = sin_ref[...]
        for hh in range(heads_per_tile):
            sl = slice(hh * HEAD_DIM, (hh + 1) * HEAD_DIM)
            a = acc[:, sl]
            y = _rms(a, gain_ref[:, sl])
            rot = pltpu.roll(y, HEAD_DIM // 2, axis=1)
            o[:, sl] = (y * cos + rot * sin).astype(BF16)

    @pl.when(j >= n_rope_tiles)
    def _v():
        o[...] = acc.astype(BF16)


def _conv_kernel(gb_ref, u_ref, up_ref, un_ref, w_ref, o_ref, *, tm, seq):
    i = pl.program_id(0)
    u = u_ref[...].astype(F32)
    w = w_ref[...]
    row = lax.broadcasted_iota(I32, u.shape, 0)
    starts_seq = (i * tm) % seq == 0
    ends_seq = ((i + 1) * tm) % seq == 0
    prev_row = up_ref[...].astype(F32)[BF16_SUBLANES - 1:BF16_SUBLANES, :]
    next_row = un_ref[...].astype(F32)[0:1, :]
    prev_row = jnp.where(starts_seq, 0.0, prev_row)
    next_row = jnp.where(ends_seq, 0.0, next_row)
    u_prev = jnp.where(row == 0, prev_row, pltpu.roll(u, 1, axis=0))
    u_next = jnp.where(row == tm - 1, next_row, pltpu.roll(u, tm - 1, axis=0))
    conv = w[0:1, :] * u_prev + w[1:2, :] * u + w[2:3, :] * u_next
    o_ref[...] = (gb_ref[...].astype(F32) * conv).astype(BF16)


def conv_gate(gb, u, w_conv, seq, tm=512):
    t, d = u.shape
    hb = tm // BF16_SUBLANES
    n_halo = t // BF16_SUBLANES
    return pl.pallas_call(
        functools.partial(_conv_kernel, tm=tm, seq=seq),
        grid=(t // tm,),
        in_specs=[pl.BlockSpec((tm, d), lambda i: (i, 0)),
                  pl.BlockSpec((tm, d), lambda i: (i, 0)),
                  pl.BlockSpec((BF16_SUBLANES, d), lambda i: (jnp.maximum(i * hb - 1, 0), 0)),
                  pl.BlockSpec((BF16_SUBLANES, d),
                               lambda i: (jnp.minimum((i + 1) * hb, n_halo - 1), 0)),
                  pl.BlockSpec((3, d), lambda i: (0, 0))],
        out_specs=pl.BlockSpec((tm, d), lambda i: (i, 0)),
        out_shape=jax.ShapeDtypeStruct((t, d), BF16),
        compiler_params=_params(12 * tm * d * 4, 1),
        name="conv_gate",
    )(gb, u, u, u, w_conv)


def _rope_kernel(pos_ref, invf_ref, sign_ref, cos_ref, sin_ref):
    ang = pos_ref[...].astype(F32) * invf_ref[...]
    cos_ref[...] = jnp.cos(ang)
    sin_ref[...] = jnp.sin(ang) * sign_ref[...]


def rope_tables(positions, tm=512):
    t = positions.shape[0]
    half = HEAD_DIM // 2
    inv_freq = ROPE_THETA ** (-jnp.arange(0, half, dtype=F32) / half)
    invf = jnp.concatenate([inv_freq, inv_freq]).reshape(1, HEAD_DIM)
    sign = jnp.concatenate([-jnp.ones((half,), F32), jnp.ones((half,), F32)]).reshape(1, HEAD_DIM)
    return pl.pallas_call(
        _rope_kernel,
        grid=(t // tm,),
        in_specs=[pl.BlockSpec((tm, 1), lambda i: (i, 0)),
                  pl.BlockSpec((1, HEAD_DIM), lambda i: (0, 0)),
                  pl.BlockSpec((1, HEAD_DIM), lambda i: (0, 0))],
        out_specs=[pl.BlockSpec((tm, HEAD_DIM), lambda i: (i, 0))] * 2,
        out_shape=[jax.ShapeDtypeStruct((t, HEAD_DIM), F32)] * 2,
        compiler_params=_params(16 * tm * HEAD_DIM * 4, 1),
        name="rope_tables",
    )(positions.reshape(t, 1), invf, sign)


def _attn_kernel(q_ref, k_ref, v_ref, sink_ref, o_ref):
    seq = q_ref.shape[0]
    n_rows = Q_PER_KV * ATTN_BLOCK
    n_keys = 3 * ATTN_BLOCK
    sink = sink_ref[...]
    q_off = lax.broadcasted_iota(I32, (n_rows, n_keys), 0) & (ATTN_BLOCK - 1)
    k_off = lax.broadcasted_iota(I32, (n_rows, n_keys), 1)

    def body(n, carry):
        q0 = pl.multiple_of(n * ATTN_BLOCK, ATTN_BLOCK)
        k0 = pl.multiple_of(jnp.clip((n - 1) * ATTN_BLOCK, 0, seq - n_keys), ATTN_BLOCK)
        qb = q_ref[pl.ds(q0, ATTN_BLOCK), :]
        qs = jnp.concatenate(
            [qb[:, g * HEAD_DIM:(g + 1) * HEAD_DIM] for g in range(Q_PER_KV)], axis=0)
        kw = k_ref[pl.ds(k0, n_keys), :]
        vw = v_ref[pl.ds(k0, n_keys), :]
        s = lax.dot_general(qs, kw, (((1,), (1,)), ((), ())), preferred_element_type=F32)
        mask = jnp.abs((k0 + k_off) - (q0 + q_off)) <= WINDOW
        s = jnp.where(mask, s, MASK_VALUE)
        m = jnp.maximum(jnp.max(s, axis=-1, keepdims=True), sink)
        p = jnp.exp(s - m)
        denom = jnp.sum(p, axis=-1, keepdims=True) + jnp.exp(sink - m)
        o = jnp.dot(p.astype(BF16), vw, preferred_element_type=F32) / denom
        for g in range(Q_PER_KV):
            o_ref[pl.ds(q0, ATTN_BLOCK), g * HEAD_DIM:(g + 1) * HEAD_DIM] = (
                o[g * ATTN_BLOCK:(g + 1) * ATTN_BLOCK, :].astype(BF16))
        return carry

    lax.fori_loop(0, seq // ATTN_BLOCK, body, 0)


def window_attention(qkv, sink, batch, seq):
    t = qkv.shape[0]
    n_heads = N_KV_HEADS * Q_PER_KV
    gw = Q_PER_KV * HEAD_DIM
    sink_col = jnp.repeat(sink.astype(F32).reshape(N_KV_HEADS, Q_PER_KV), ATTN_BLOCK, axis=1)
    sink_col = sink_col.reshape(N_KV_HEADS, Q_PER_KV * ATTN_BLOCK, 1)
    return pl.pallas_call(
        _attn_kernel,
        grid=(batch, N_KV_HEADS),
        in_specs=[pl.BlockSpec((seq, gw), lambda b, h: (b, h)),
                  pl.BlockSpec((seq, HEAD_DIM), lambda b, h: (b, n_heads + h)),
                  pl.BlockSpec((seq, HEAD_DIM), lambda b, h: (b, n_heads + N_KV_HEADS + h)),
                  pl.BlockSpec((None, Q_PER_KV * ATTN_BLOCK, 1), lambda b, h: (h, 0, 0))],
        out_specs=pl.BlockSpec((seq, gw), lambda b, h: (b, h)),
        out_shape=jax.ShapeDtypeStruct((t, n_heads * HEAD_DIM), BF16),
        compiler_params=_params(6 * seq * gw * 2 + 16 * 1024 * 1024, 2),
        name="window_attention",
    )(qkv, qkv, qkv, sink_col)


def _route_kernel(x_ref, g_ref, r_ref, meta_ref, gate_ref, cnt_ref, carry_ref, *, tm):
    i = pl.program_id(0)

    @pl.when(i == 0)
    def _init():
        carry_ref[...] = jnp.zeros(carry_ref.shape, carry_ref.dtype)

    h = _rms(x_ref[...], g_ref[...])
    logits = jnp.dot(h, r_ref[...], preferred_element_type=F32,
                     precision=lax.Precision.HIGHEST)
    lane = lax.broadcasted_iota(I32, logits.shape, 1)
    lane_f = lane.astype(F32)
    neg_inf = jnp.float32(-jnp.inf)
    logits = jnp.where(lane < N_EXPERTS, logits, neg_inf)
    v1 = jnp.max(logits, axis=-1, keepdims=True)
    i1 = jnp.min(jnp.where(logits == v1, lane_f, float(LANES)), axis=-1, keepdims=True)
    rest = jnp.where(lane_f == i1, neg_inf, logits)
    v2 = jnp.max(rest, axis=-1, keepdims=True)
    i2 = jnp.min(jnp.where(rest == v2, lane_f, float(LANES)), axis=-1, keepdims=True)
    e21 = jnp.exp(v2 - v1)
    g1 = 1.0 / (1.0 + e21)
    g2 = e21 / (1.0 + e21)

    sel1 = lane_f == i1
    sel2 = lane_f == i2
    onehot = jnp.logical_or(sel1, sel2)
    r_io = lax.broadcasted_iota(I32, (tm, tm), 0)
    c_io = lax.broadcasted_iota(I32, (tm, tm), 1)
    strict_lower = (c_io < r_io).astype(BF16)
    carry = carry_ref[...]
    before = jnp.dot(strict_lower, onehot.astype(BF16), preferred_element_type=F32) + carry
    rank1 = jnp.sum(jnp.where(sel1, before, 0.0), axis=-1, keepdims=True)
    rank2 = jnp.sum(jnp.where(sel2, before, 0.0), axis=-1, keepdims=True)
    carry = carry + jnp.sum(onehot.astype(F32), axis=0, keepdims=True)
    carry_ref[...] = carry
    cnt_ref[...] = carry.astype(I32)

    meta = jnp.where(lane == 0, i1, jnp.where(lane == 1, i2,
           jnp.where(lane == 2, rank1, jnp.where(lane == 3, rank2, 0.0))))
    meta_ref[...] = meta.astype(I32)
    gate_ref[...] = jnp.where(lane == 0, g1, jnp.where(lane == 1, g2, 0.0))


def moe_route(x, g, w_router, tm=512):
    t, d = x.shape
    r_pad = jnp.zeros((d, LANES), F32).at[:, :N_EXPERTS].set(w_router)
    return pl.pallas_call(
        functools.partial(_route_kernel, tm=tm),
        grid=(t // tm,),
        in_specs=[pl.BlockSpec((tm, d), lambda i: (i, 0)),
                  pl.BlockSpec((1, d), lambda i: (0, 0)),
                  pl.BlockSpec((d, LANES), lambda i: (0, 0))],
        out_specs=[pl.BlockSpec((tm, LANES), lambda i: (i, 0)),
                   pl.BlockSpec((tm, LANES), lambda i: (i, 0)),
                   pl.BlockSpec((1, LANES), lambda i: (0, 0))],
        out_shape=[jax.ShapeDtypeStruct((t, LANES), I32),
                   jax.ShapeDtypeStruct((t, LANES), F32),
                   jax.ShapeDtypeStruct((1, LANES), I32)],
        scratch_shapes=[pltpu.VMEM((1, LANES), F32)],
        compiler_params=_params(8 * tm * d * 4 + 8 * 1024 * 1024, 1),
        name="moe_route",
    )(x, g.reshape(1, d), r_pad)


def _invmap_kernel(pos_ref, src_ref, *, n_tokens, n_slots):
    def clear(p, c):
        src_ref[p] = 0
        return c
    lax.fori_loop(0, n_slots, clear, 0)

    def fill(t, c):
        src_ref[pos_ref[2 * t]] = t
        src_ref[pos_ref[2 * t + 1]] = t
        return c
    lax.fori_loop(0, n_tokens, fill, 0)


def moe_slot_sources(pos_flat, n_tokens, n_slots):
    return pl.pallas_call(
        functools.partial(_invmap_kernel, n_tokens=n_tokens, n_slots=n_slots),
        in_specs=[pl.BlockSpec(memory_space=pltpu.SMEM)],
        out_specs=pl.BlockSpec(memory_space=pltpu.SMEM),
        out_shape=jax.ShapeDtypeStruct((n_slots,), I32),
        name="moe_slot_sources",
    )(pos_flat)


def _dispatch_kernel(src_ref, x_hbm, g_ref, o_ref, buf, sem, *, tg):
    base = pl.program_id(0) * tg

    def issue(r, c):
        t = src_ref[base + r]
        pltpu.make_async_copy(x_hbm.at[pl.ds(t, 1), :], buf.at[pl.ds(r, 1), :], sem).start()
        return c
    lax.fori_loop(0, tg, issue, 0)
    pltpu.make_async_copy(x_hbm.at[pl.ds(0, tg), :], buf, sem).wait()
    o_ref[...] = _rms(buf[...], g_ref[...]).astype(BF16)


def moe_dispatch(x, g, src, tg=256):
    t, d = x.shape
    n_slots = src.shape[0]
    return pl.pallas_call(
        functools.partial(_dispatch_kernel, tg=tg),
        grid_spec=pltpu.PrefetchScalarGridSpec(
            num_scalar_prefetch=1,
            grid=(n_slots // tg,),
            in_specs=[pl.BlockSpec(memory_space=pl.ANY),
                      pl.BlockSpec((1, d), lambda i, src: (0, 0))],
            out_specs=pl.BlockSpec((tg, d), lambda i, src: (i, 0)),
            scratch_shapes=[pltpu.VMEM((tg, d), F32), pltpu.SemaphoreType.DMA(())]),
        out_shape=jax.ShapeDtypeStruct((n_slots, d), BF16),
        compiler_params=_params(8 * tg * d * 4 + 4 * 1024 * 1024, 1),
        name="moe_dispatch",
    )(src, x, g.reshape(1, d))


def _combine_kernel(pos_ref, x_ref, gate_ref, y_hbm, g_ref, *refs, tc, with_norm):
    if with_norm:
        xo_ref, ho_ref, buf, sem = refs
    else:
        xo_ref, buf, sem = refs
    base = pl.program_id(0) * tc

    def issue(r, c):
        p1 = pos_ref[2 * (base + r)]
        p2 = pos_ref[2 * (base + r) + 1]
        pltpu.make_async_copy(y_hbm.at[pl.ds(p1, 1), :], buf.at[pl.ds(r, 1), :], sem).start()
        pltpu.make_async_copy(y_hbm.at[pl.ds(p2, 1), :], buf.at[pl.ds(tc + r, 1), :], sem).start()
        return c
    lax.fori_loop(0, tc, issue, 0)
    pltpu.make_async_copy(y_hbm.at[pl.ds(0, 2 * tc), :], buf, sem).wait()
    gates = gate_ref[...]
    y = gates[:, 0:1] * buf[0:tc, :] + gates[:, 1:2] * buf[tc:2 * tc, :]
    xn = x_ref[...] + y
    xo_ref[...] = xn
    if with_norm:
        ho_ref[...] = _rms(xn, g_ref[...]).astype(BF16)


def moe_combine(x, gates, y, pos_flat, g_next, tc=256):
    t, d = x.shape
    with_norm = g_next is not None
    g_arr = (g_next if with_norm else jnp.ones((d,), F32)).reshape(1, d)
    out_specs = [pl.BlockSpec((tc, d), lambda i, pos: (i, 0))]
    out_shape = [jax.ShapeDtypeStruct((t, d), F32)]
    if with_norm:
        out_specs.append(pl.BlockSpec((tc, d), lambda i, pos: (i, 0)))
        out_shape.append(jax.ShapeDtypeStruct((t, d), BF16))
    outs = pl.pallas_call(
        functools.partial(_combine_kernel, tc=tc, with_norm=with_norm),
        grid_spec=pltpu.PrefetchScalarGridSpec(
            num_scalar_prefetch=1,
            grid=(t // tc,),
            in_specs=[pl.BlockSpec((tc, d), lambda i, pos: (i, 0)),
                      pl.BlockSpec((tc, LANES), lambda i, pos: (i, 0)),
                      pl.BlockSpec(memory_space=pl.ANY),
                      pl.BlockSpec((1, d), lambda i, pos: (0, 0))],
            out_specs=out_specs,
            scratch_shapes=[pltpu.VMEM((2 * tc, d), F32), pltpu.SemaphoreType.DMA(())]),
        out_shape=out_shape,
        compiler_params=_params(16 * tc * d * 4 + 4 * 1024 * 1024, 1),
        name="moe_combine",
    )(pos_flat, x, gates, y, g_arr)
    return outs if with_norm else (outs[0], None)


def moe_layer(x, g_ffn, w_router, w13, w2, g_next):
    t, d = x.shape
    d_ff = w2.shape[1]
    tm = MOE_TM
    n_tiles = (2 * t) // tm + N_EXPERTS
    n_slots = n_tiles * tm

    meta, gates, counts = moe_route(x, g_ffn, w_router)
    cnt = counts[0, :N_EXPERTS]
    tiles_per = (cnt + tm - 1) // tm
    tile_end = jnp.cumsum(tiles_per)
    row_start = (tile_end - tiles_per) * tm
    e1, e2, r1, r2 = meta[:, 0], meta[:, 1], meta[:, 2], meta[:, 3]
    pos = jnp.stack([row_start[e1] + r1, row_start[e2] + r2], axis=1).reshape(-1).astype(I32)
    tile_id = jnp.arange(n_tiles, dtype=I32)
    used = tile_end[-1]
    tv = (tile_id < used).astype(I32)
    te = jnp.searchsorted(tile_end, jnp.minimum(tile_id, used - 1), side="right").astype(I32)

    src = moe_slot_sources(pos, t, n_slots)
    xs = moe_dispatch(x, g_ffn, src)

    tn13 = 512
    (act,) = grouped_matmul(
        xs, [(w13, 0), (w13, d_ff // tn13)], te, tv, tm=tm, tn=tn13,
        n_col_tiles=d_ff // tn13, extras=[], out_dtypes=[BF16], out_cols=d_ff,
        epilogue=_epi_swiglu, name="moe_w13")
    tn2 = 512
    (y,) = grouped_matmul(
        act, [(w2, 0)], jnp.repeat(te, 2), jnp.repeat(tv, 2), tm=tm // 2, tn=tn2,
        n_col_tiles=d // tn2, extras=[], out_dtypes=[F32], out_cols=d,
        epilogue=_epi_plain, name="moe_w2")
    return moe_combine(x, gates, y, pos, g_next)


def conv_layer(x, h, w_in, w_conv, w_out, seq):
    t, d = x.shape
    tn = 512
    te, tv = _dense_tiles(t, 1024)
    nt = d // tn
    gb, u = grouped_matmul(
        h, [(w_in[None], 0), (w_in[None], nt), (w_in[None], 2 * nt)], te, tv, tm=1024, tn=tn,
        n_col_tiles=nt, extras=[], out_dtypes=[BF16, BF16], out_cols=d,
        epilogue=_epi_conv_in, name="conv_in")
    z = conv_gate(gb, u, w_conv, seq)
    (x,) = grouped_matmul(
        z, [(w_out[None], 0)], te, tv, tm=1024, tn=tn, n_col_tiles=nt,
        extras=[(x, (1024, tn), lambda j, i: (i, j))], out_dtypes=[F32], out_cols=d,
        epilogue=_epi_residual, name="conv_out")
    return x


def dense_ffn(x, h, w13, w2):
    t, d = x.shape
    d_ff = w2.shape[0]
    tn = 512
    te, tv = _dense_tiles(t, 1024)
    (act,) = grouped_matmul(
        h, [(w13[None], 0), (w13[None], d_ff // tn)], te, tv, tm=1024, tn=tn,
        n_col_tiles=d_ff // tn, extras=[], out_dtypes=[BF16], out_cols=d_ff,
        epilogue=_epi_swiglu, name="ffn_w13")
    te2, tv2 = _dense_tiles(t, 256)
    (x,) = grouped_matmul(
        act, [(w2[None], 0)], te2, tv2, tm=256, tn=tn, n_col_tiles=d // tn,
        extras=[(x, (256, tn), lambda j, i: (i, j))], out_dtypes=[F32], out_cols=d,
        epilogue=_epi_residual, name="ffn_w2")
    return x


def attention_layer(x, h, cos, sin, w_qkv, q_gain, k_gain, sink, w_out, batch, seq):
    t, d = x.shape
    n_heads = N_KV_HEADS * Q_PER_KV
    tn = 512
    heads_per_tile = tn // HEAD_DIM
    qkv_dim = w_qkv.shape[1]
    n_rope_tiles = (n_heads + N_KV_HEADS) // heads_per_tile
    gain_cols = jnp.concatenate([
        jnp.tile(q_gain.astype(F32) * (1.0 / math.sqrt(HEAD_DIM)), n_heads),
        jnp.tile(k_gain.astype(F32), N_KV_HEADS),
        jnp.ones((N_KV_HEADS * HEAD_DIM,), F32)]).reshape(1, qkv_dim)
    te, tv = _dense_tiles(t, 1024)
    (qkv,) = grouped_matmul(
        h, [(w_qkv[None], 0)], te, tv, tm=1024, tn=tn, n_col_tiles=qkv_dim // tn,
        extras=[(cos, (1024, HEAD_DIM), lambda j, i: (i, 0)),
                (sin, (1024, HEAD_DIM), lambda j, i: (i, 0)),
                (gain_cols, (1, tn), lambda j, i: (0, j))],
        out_dtypes=[BF16], out_cols=qkv_dim,
        epilogue=functools.partial(_epi_qkv, heads_per_tile=heads_per_tile,
                                   n_rope_tiles=n_rope_tiles),
        name="attn_qkv")
    o = window_attention(qkv, sink, batch, seq)
    (x,) = grouped_matmul(
        o, [(w_out[None], 0)], te, tv, tm=1024, tn=tn, n_col_tiles=d // tn,
        extras=[(x, (1024, tn), lambda j, i: (i, j))], out_dtypes=[F32], out_cols=d,
        epilogue=_epi_residual, name="attn_out")
    return x


def kernel(x, positions, norm_mix, norm_ffn, conv_in, conv_w, conv_out, attn_qkv, q_norm, k_norm,
           attn_sink, attn_out, ffn_w13, ffn_w2, router, moe_w13, moe_w2):
    batch, seq, d = x.shape
    depth = norm_mix.shape[0]
    t = batch * seq
    x = x.reshape(t, d)
    cos, sin = rope_tables(positions.reshape(t))
    h = rms_norm_bf16(x, norm_mix[0])
    for i in range(depth):
        j = i // 2
        if i % 2 == 0:
            x = conv_layer(x, h, conv_in[j], conv_w[j], conv_out[j], seq)
            h = rms_norm_bf16(x, norm_ffn[i])
            x = dense_ffn(x, h, ffn_w13[j], ffn_w2[j])
            h = rms_norm_bf16(x, norm_mix[i + 1])
        else:
            x = attention_layer(x, h, cos, sin, attn_qkv[j], q_norm[j], k_norm[j],
                                attn_sink[j], attn_out[j], batch, seq)
            g_next = norm_mix[i + 1] if i + 1 < depth else None
            x, h = moe_layer(x, norm_ffn[i], router[j], moe_w13[j], moe_w2[j], g_next)
    return x.reshape(batch, seq, d)
```

```python
import functools
import math

import jax
import jax.numpy as jnp
from jax import lax
from jax.experimental import pallas as pl
from jax.experimental.pallas import tpu as pltpu

F32 = jnp.float32
BF16 = jnp.bfloat16
I32 = jnp.int32

HEAD_DIM = 128
N_KV_HEADS = 4
Q_PER_KV = 4
WINDOW = 128
ATTN_BLOCK = 128
ROPE_THETA = 10000.0
N_EXPERTS = 8
RMS_EPS = 1e-6
MASK_VALUE = -1e30

LANES = 128
BF16_SUBLANES = 16
VMEM_BUDGET = 60000 * 1024

MOE_TM = 512
MOE_SUB = 256


def _params(vmem_bytes, n_grid):
    return pltpu.CompilerParams(
        dimension_semantics=("arbitrary",) * n_grid,
        vmem_limit_bytes=int(min(vmem_bytes, VMEM_BUDGET)))


def _nbytes(shape, dtype):
    return math.prod(shape) * jnp.dtype(dtype).itemsize


def _rms(x, g):
    ms = jnp.mean(x * x, axis=-1, keepdims=True)
    return x * lax.rsqrt(ms + RMS_EPS) * g


def _norm_kernel(x_ref, g_ref, o_ref):
    o_ref[...] = _rms(x_ref[...], g_ref[...]).astype(o_ref.dtype)


def rms_norm_bf16(x, g, tm=512):
    t, d = x.shape
    return pl.pallas_call(
        _norm_kernel,
        grid=(t // tm,),
        in_specs=[pl.BlockSpec((tm, d), lambda i: (i, 0)),
                  pl.BlockSpec((1, d), lambda i: (0, 0))],
        out_specs=pl.BlockSpec((tm, d), lambda i: (i, 0)),
        out_shape=jax.ShapeDtypeStruct((t, d), BF16),
        compiler_params=_params(6 * tm * d * 4, 1),
        name="rms_norm",
    )(x, g.reshape(1, d))


def _gmm_kernel(te_ref, tr_ref, nx_ref, x_ref, *refs, n_w, n_extra, n_out, col_offs, tn,
                sub, epilogue):
    w_refs = refs[:n_w]
    e_refs = refs[n_w:n_w + n_extra]
    o_refs = refs[n_w + n_extra:n_w + n_extra + n_out]
    stage, wb, sem = refs[n_w + n_extra + n_out:]
    j = pl.program_id(0)
    i = pl.program_id(1)
    n_j = pl.num_programs(0)
    tm = x_ref.shape[0]
    cur = te_ref[i]
    first_of_group = jnp.logical_or(i == 0, cur != te_ref[jnp.maximum(i - 1, 0)])

    def weight_copy(v, group, jj):
        col = pl.multiple_of((jj + col_offs[v]) * tn, tn)
        return pltpu.make_async_copy(
            w_refs[v].at[group, :, pl.ds(col, tn)], stage.at[v], sem.at[v])

    @pl.when(jnp.logical_and(j == 0, i == 0))
    def _prime():
        for v in range(n_w):
            weight_copy(v, cur, j).start()

    @pl.when(first_of_group)
    def _load_weights():
        for v in range(n_w):
            weight_copy(v, cur, j).wait()
            wb[v] = stage[v].astype(BF16)
        nxt = nx_ref[i]

        @pl.when(nxt >= 0)
        def _():
            for v in range(n_w):
                weight_copy(v, nxt, j).start()

        @pl.when(jnp.logical_and(nxt < 0, j + 1 < n_j))
        def _():
            for v in range(n_w):
                weight_copy(v, te_ref[0], j + 1).start()

    for s in range(tm // sub):
        rows = slice(s * sub, (s + 1) * sub)

        @pl.when(tr_ref[i] > s * sub)
        def _compute():
            xb = x_ref[rows, :]
            accs = [jnp.dot(xb, wb[v], preferred_element_type=F32) for v in range(n_w)]
            epilogue(j, rows, accs, e_refs, o_refs)

        @pl.when(tr_ref[i] <= s * sub)
        def _skip():
            for o in o_refs:
                o[rows, :] = jnp.zeros((sub, o.shape[1]), o.dtype)


def grouped_matmul(x, weights, te, tr, *, tm, sub, tn, n_col_tiles, extras, out_dtypes,
                   out_cols, epilogue, name):
    rows, k = x.shape
    n_w, n_extra, n_out = len(weights), len(extras), len(out_dtypes)
    later = jnp.where(te[None, :] > te[:, None], te[None, :], jnp.iinfo(jnp.int32).max)
    nx = jnp.min(later, axis=1)
    nx = jnp.where(nx == jnp.iinfo(jnp.int32).max, -1, nx).astype(I32)

    in_specs = [pl.BlockSpec((tm, k), lambda j, i, *_: (i, 0))]
    in_specs += [pl.BlockSpec(memory_space=pl.ANY) for _ in weights]
    for _, bshape, imap in extras:
        in_specs.append(pl.BlockSpec(bshape, lambda j, i, *_, imap=imap: imap(j, i)))
    out_specs = [pl.BlockSpec((tm, tn), lambda j, i, *_: (i, j)) for _ in out_dtypes]
    out_shape = [jax.ShapeDtypeStruct((rows, out_cols), dt) for dt in out_dtypes]

    vmem = 2 * _nbytes((tm, k), x.dtype)
    vmem += n_w * (_nbytes((k, tn), F32) + _nbytes((k, tn), BF16))
    vmem += sum(2 * _nbytes(b, a.dtype) for a, b, _ in extras)
    vmem += sum(2 * _nbytes((tm, tn), dt) for dt in out_dtypes)
    vmem += (n_w + 3) * _nbytes((sub, tn), F32)
    vmem += _nbytes((sub, k), x.dtype) + 6 * 1024 * 1024

    return pl.pallas_call(
        functools.partial(_gmm_kernel, n_w=n_w, n_extra=n_extra, n_out=n_out,
                          col_offs=tuple(off for _, off in weights), tn=tn, sub=sub,
                          epilogue=epilogue),
        grid_spec=pltpu.PrefetchScalarGridSpec(
            num_scalar_prefetch=3,
            grid=(n_col_tiles, rows // tm),
            in_specs=in_specs,
            out_specs=out_specs,
            scratch_shapes=[pltpu.VMEM((n_w, k, tn), F32),
                            pltpu.VMEM((n_w, k, tn), BF16),
                            pltpu.SemaphoreType.DMA((n_w,))]),
        out_shape=out_shape,
        compiler_params=_params(vmem, 2),
        name=name,
    )(te, tr, nx, x, *[w for w, _ in weights], *[a for a, _, _ in extras])


def _dense_tiles(rows, tm, group):
    n = rows // tm
    return jnp.full((n,), group, I32), jnp.full((n,), tm, I32)


def _epi_swiglu(j, rows, accs, e_refs, o_refs):
    g, u = accs
    o_refs[0][rows, :] = (g * jax.nn.sigmoid(g) * u).astype(BF16)


def _epi_residual(j, rows, accs, e_refs, o_refs):
    o_refs[0][rows, :] = e_refs[0][rows, :] + accs[0]


def _epi_plain(j, rows, accs, e_refs, o_refs):
    o_refs[0][rows, :] = accs[0]


def _epi_conv_in(j, rows, accs, e_refs, o_refs):
    gate_b, gate_c, val = accs
    o_refs[0][rows, :] = gate_b.astype(BF16)
    o_refs[1][rows, :] = (gate_c * val).astype(BF16)


def _epi_qkv(j, rows, accs, e_refs, o_refs, *, heads_per_tile, n_rope_tiles):
    acc = accs[0]
    cos_ref, sin_ref, gain_ref = e_refs
    o = o_refs[0]

    @pl.when(j < n_rope_tiles)
    def _qk():
        cos = cos_ref[rows, :]
        sin = sin_ref[rows, :]
        for hh in range(heads_per_tile):
            sl = slice(hh * HEAD_DIM, (hh + 1) * HEAD_DIM)
            y = _rms(acc[:, sl], gain_ref[:, sl])
            rot = pltpu.roll(y, HEAD_DIM // 2, axis=1)
            o[rows, sl] = (y * cos + rot * sin).astype(BF16)

    @pl.when(j >= n_rope_tiles)
    def _v():
        o[rows, :] = acc.astype(BF16)


def _conv_kernel(gb_ref, u_ref, up_ref, un_ref, w_ref, o_ref, *, tm, seq):
    i = pl.program_id(0)
    u = u_ref[...].astype(F32)
    w = w_ref[...]
    row = lax.broadcasted_iota(I32, u.shape, 0)
    starts_seq = (i * tm) % seq == 0
    ends_seq = ((i + 1) * tm) % seq == 0
    prev_row = up_ref[...].astype(F32)[BF16_SUBLANES - 1:BF16_SUBLANES, :]
    next_row = un_ref[...].astype(F32)[0:1, :]
    prev_row = jnp.where(starts_seq, 0.0, prev_row)
    next_row = jnp.where(ends_seq, 0.0, next_row)
    u_prev = jnp.where(row == 0, prev_row, pltpu.roll(u, 1, axis=0))
    u_next = jnp.where(row == tm - 1, next_row, pltpu.roll(u, tm - 1, axis=0))
    conv = w[0:1, :] * u_prev + w[1:2, :] * u + w[2:3, :] * u_next
    o_ref[...] = (gb_ref[...].astype(F32) * conv).astype(BF16)


def conv_gate(gb, u, w_conv, seq, tm=512):
    t, d = u.shape
    hb = tm // BF16_SUBLANES
    n_halo = t // BF16_SUBLANES
    return pl.pallas_call(
        functools.partial(_conv_kernel, tm=tm, seq=seq),
        grid=(t // tm,),
        in_specs=[pl.BlockSpec((tm, d), lambda i: (i, 0)),
                  pl.BlockSpec((tm, d), lambda i: (i, 0)),
                  pl.BlockSpec((BF16_SUBLANES, d), lambda i: (jnp.maximum(i * hb - 1, 0), 0)),
                  pl.BlockSpec((BF16_SUBLANES, d),
                               lambda i: (jnp.minimum((i + 1) * hb, n_halo - 1), 0)),
                  pl.BlockSpec((3, d), lambda i: (0, 0))],
        out_specs=pl.BlockSpec((tm, d), lambda i: (i, 0)),
        out_shape=jax.ShapeDtypeStruct((t, d), BF16),
        compiler_params=_params(12 * tm * d * 4, 1),
        name="conv_gate",
    )(gb, u, u, u, w_conv)


def _rope_kernel(pos_ref, invf_ref, sign_ref, cos_ref, sin_ref):
    ang = pos_ref[...].astype(F32) * invf_ref[...]
    cos_ref[...] = jnp.cos(ang)
    sin_ref[...] = jnp.sin(ang) * sign_ref[...]


def rope_tables(positions, tm=512):
    t = positions.shape[0]
    half = HEAD_DIM // 2
    inv_freq = ROPE_THETA ** (-jnp.arange(0, half, dtype=F32) / half)
    invf = jnp.concatenate([inv_freq, inv_freq]).reshape(1, HEAD_DIM)
    sign = jnp.concatenate([-jnp.ones((half,), F32), jnp.ones((half,), F32)]).reshape(1, HEAD_DIM)
    return pl.pallas_call(
        _rope_kernel,
        grid=(t // tm,),
        in_specs=[pl.BlockSpec((tm, 1), lambda i: (i, 0)),
                  pl.BlockSpec((1, HEAD_DIM), lambda i: (0, 0)),
                  pl.BlockSpec((1, HEAD_DIM), lambda i: (0, 0))],
        out_specs=[pl.BlockSpec((tm, HEAD_DIM), lambda i: (i, 0))] * 2,
        out_shape=[jax.ShapeDtypeStruct((t, HEAD_DIM), F32)] * 2,
        compiler_params=_params(16 * tm * HEAD_DIM * 4, 1),
        name="rope_tables",
    )(positions.reshape(t, 1), invf, sign)


ATTN_ROWS = Q_PER_KV * ATTN_BLOCK
ATTN_KEYS = 3 * ATTN_BLOCK
ATTN_UNROLL = 2


def _attn_kernel(q_ref, k_ref, v_ref, sink_ref, o_ref, bias_ref):
    seq = q_ref.shape[0]
    sink = sink_ref[...]
    q_off = lax.broadcasted_iota(I32, (ATTN_ROWS, ATTN_KEYS), 0) & (ATTN_BLOCK - 1)
    k_off = lax.broadcasted_iota(I32, (ATTN_ROWS, ATTN_KEYS), 1)
    for c in range(3):
        inside = jnp.abs(k_off - q_off - c * ATTN_BLOCK) <= WINDOW
        bias_ref[c] = jnp.where(inside, 0.0, MASK_VALUE)

    def block(n):
        q0 = pl.multiple_of(n * ATTN_BLOCK, ATTN_BLOCK)
        k0 = pl.multiple_of(jnp.clip((n - 1) * ATTN_BLOCK, 0, seq - ATTN_KEYS), ATTN_BLOCK)
        qb = q_ref[pl.ds(q0, ATTN_BLOCK), :]
        qs = jnp.concatenate(
            [qb[:, g * HEAD_DIM:(g + 1) * HEAD_DIM] for g in range(Q_PER_KV)], axis=0)
        kw = k_ref[pl.ds(k0, ATTN_KEYS), :]
        vw = v_ref[pl.ds(k0, ATTN_KEYS), :]
        s = lax.dot_general(qs, kw, (((1,), (1,)), ((), ())), preferred_element_type=F32)
        s = s + bias_ref[(q0 - k0) // ATTN_BLOCK]
        m = jnp.maximum(jnp.max(s, axis=-1, keepdims=True), sink)
        p = jnp.exp(s - m)
        denom = jnp.sum(p, axis=-1, keepdims=True) + jnp.exp(sink - m)
        o = jnp.dot(p.astype(BF16), vw, preferred_element_type=F32) / denom
        for g in range(Q_PER_KV):
            o_ref[pl.ds(q0, ATTN_BLOCK), g * HEAD_DIM:(g + 1) * HEAD_DIM] = (
                o[g * ATTN_BLOCK:(g + 1) * ATTN_BLOCK, :].astype(BF16))

    def body(n2, carry):
        for r in range(ATTN_UNROLL):
            block(n2 * ATTN_UNROLL + r)
        return carry

    lax.fori_loop(0, seq // (ATTN_BLOCK * ATTN_UNROLL), body, 0)


def window_attention(qkv, sink, batch, seq):
    t = qkv.shape[0]
    n_heads = N_KV_HEADS * Q_PER_KV
    gw = Q_PER_KV * HEAD_DIM
    sink_col = jnp.repeat(sink.astype(F32).reshape(N_KV_HEADS, Q_PER_KV), ATTN_BLOCK, axis=1)
    sink_col = sink_col.reshape(N_KV_HEADS, ATTN_ROWS, 1)
    return pl.pallas_call(
        _attn_kernel,
        grid=(batch, N_KV_HEADS),
        in_specs=[pl.BlockSpec((seq, gw), lambda b, h: (b, h)),
                  pl.BlockSpec((seq, HEAD_DIM), lambda b, h: (b, n_heads + h)),
                  pl.BlockSpec((seq, HEAD_DIM), lambda b, h: (b, n_heads + N_KV_HEADS + h)),
                  pl.BlockSpec((None, ATTN_ROWS, 1), lambda b, h: (h, 0, 0))],
        out_specs=pl.BlockSpec((seq, gw), lambda b, h: (b, h)),
        out_shape=jax.ShapeDtypeStruct((t, n_heads * HEAD_DIM), BF16),
        scratch_shapes=[pltpu.VMEM((3, ATTN_ROWS, ATTN_KEYS), F32)],
        compiler_params=_params(6 * seq * gw * 2 + 24 * 1024 * 1024, 2),
        name="window_attention",
    )(qkv, qkv, qkv, sink_col)


def _route_kernel(x_ref, g_ref, r_ref, h_ref, meta_ref, gate_ref, cnt_ref, carry_ref, *, tm):
    i = pl.program_id(0)

    @pl.when(i == 0)
    def _init():
        carry_ref[...] = jnp.zeros(carry_ref.shape, carry_ref.dtype)

    h = _rms(x_ref[...], g_ref[...])
    h_ref[...] = h.astype(BF16)
    logits = jnp.dot(h, r_ref[...], preferred_element_type=F32,
                     precision=lax.Precision.HIGHEST)
    lane = lax.broadcasted_iota(I32, logits.shape, 1)
    lane_f = lane.astype(F32)
    neg_inf = jnp.float32(-jnp.inf)
    logits = jnp.where(lane < N_EXPERTS, logits, neg_inf)
    v1 = jnp.max(logits, axis=-1, keepdims=True)
    i1 = jnp.min(jnp.where(logits == v1, lane_f, float(LANES)), axis=-1, keepdims=True)
    rest = jnp.where(lane_f == i1, neg_inf, logits)
    v2 = jnp.max(rest, axis=-1, keepdims=True)
    i2 = jnp.min(jnp.where(rest == v2, lane_f, float(LANES)), axis=-1, keepdims=True)
    e21 = jnp.exp(v2 - v1)
    g1 = 1.0 / (1.0 + e21)
    g2 = e21 / (1.0 + e21)

    sel1 = lane_f == i1
    sel2 = lane_f == i2
    onehot = jnp.logical_or(sel1, sel2)
    r_io = lax.broadcasted_iota(I32, (tm, tm), 0)
    c_io = lax.broadcasted_iota(I32, (tm, tm), 1)
    strict_lower = (c_io < r_io).astype(BF16)
    carry = carry_ref[...]
    before = jnp.dot(strict_lower, onehot.astype(BF16), preferred_element_type=F32) + carry
    rank1 = jnp.sum(jnp.where(sel1, before, 0.0), axis=-1, keepdims=True)
    rank2 = jnp.sum(jnp.where(sel2, before, 0.0), axis=-1, keepdims=True)
    carry = carry + jnp.sum(onehot.astype(F32), axis=0, keepdims=True)
    carry_ref[...] = carry
    cnt_ref[...] = carry.astype(I32)

    meta = jnp.where(lane == 0, i1, jnp.where(lane == 1, i2,
           jnp.where(lane == 2, rank1, jnp.where(lane == 3, rank2, 0.0))))
    meta_ref[...] = meta.astype(I32)
    gate_ref[...] = jnp.where(lane == 0, g1, jnp.where(lane == 1, g2, 0.0))


def moe_route(x, g, w_router, tm=512):
    t, d = x.shape
    r_pad = jnp.zeros((d, LANES), F32).at[:, :N_EXPERTS].set(w_router)
    return pl.pallas_call(
        functools.partial(_route_kernel, tm=tm),
        grid=(t // tm,),
        in_specs=[pl.BlockSpec((tm, d), lambda i: (i, 0)),
                  pl.BlockSpec((1, d), lambda i: (0, 0)),
                  pl.BlockSpec((d, LANES), lambda i: (0, 0))],
        out_specs=[pl.BlockSpec((tm, d), lambda i: (i, 0)),
                   pl.BlockSpec((tm, LANES), lambda i: (i, 0)),
                   pl.BlockSpec((tm, LANES), lambda i: (i, 0)),
                   pl.BlockSpec((1, LANES), lambda i: (0, 0))],
        out_shape=[jax.ShapeDtypeStruct((t, d), BF16),
                   jax.ShapeDtypeStruct((t, LANES), I32),
                   jax.ShapeDtypeStruct((t, LANES), F32),
                   jax.ShapeDtypeStruct((1, LANES), I32)],
        scratch_shapes=[pltpu.VMEM((1, LANES), F32)],
        compiler_params=_params(10 * tm * d * 4 + 8 * 1024 * 1024, 1),
        name="moe_route",
    )(x, g.reshape(1, d), r_pad)


DISPATCH_BLOCK = 512
DISPATCH_UNROLL = 8


def _dispatch_kernel(pos_ref, h_hbm, zeros_hbm, xs_hbm, sem, *, n_tokens):
    del zeros_hbm
    n_blocks = n_tokens // DISPATCH_BLOCK

    def issue(b):
        def body(r, c):
            t = b * DISPATCH_BLOCK + r
            for k in range(2):
                pltpu.make_async_copy(h_hbm.at[t], xs_hbm.at[pos_ref[2 * t + k]], sem).start()
            return c
        lax.fori_loop(0, DISPATCH_BLOCK, body, 0, unroll=DISPATCH_UNROLL)

    def wait_block():
        n = 2 * DISPATCH_BLOCK
        pltpu.make_async_copy(h_hbm.at[pl.ds(0, n)], xs_hbm.at[pl.ds(0, n)], sem).wait()

    issue(0)

    def outer(b, c):
        issue(b)
        wait_block()
        return c
    lax.fori_loop(1, n_blocks, outer, 0)
    wait_block()


def moe_dispatch(h, pos_flat, n_slots):
    t, d = h.shape
    h3 = h.reshape(t, d // LANES, LANES)
    zeros = jnp.zeros((n_slots, d // LANES, LANES), BF16)
    xs = pl.pallas_call(
        functools.partial(_dispatch_kernel, n_tokens=t),
        in_specs=[pl.BlockSpec(memory_space=pltpu.SMEM),
                  pl.BlockSpec(memory_space=pl.ANY),
                  pl.BlockSpec(memory_space=pl.ANY)],
        out_specs=pl.BlockSpec(memory_space=pl.ANY),
        out_shape=jax.ShapeDtypeStruct(zeros.shape, BF16),
        scratch_shapes=[pltpu.SemaphoreType.DMA(())],
        input_output_aliases={2: 0},
        name="moe_dispatch",
    )(pos_flat, h3, zeros)
    return xs.reshape(n_slots, d)


def _combine_kernel(pos_ref, x_ref, gate_ref, y_hbm, g_ref, *refs, tc, with_norm):
    if with_norm:
        xo_ref, ho_ref, buf, sem = refs
    else:
        xo_ref, buf, sem = refs
    base = pl.program_id(0) * tc

    def issue(r, c):
        p1 = pos_ref[2 * (base + r)]
        p2 = pos_ref[2 * (base + r) + 1]
        pltpu.make_async_copy(y_hbm.at[pl.ds(p1, 1), :], buf.at[pl.ds(r, 1), :], sem).start()
        pltpu.make_async_copy(y_hbm.at[pl.ds(p2, 1), :], buf.at[pl.ds(tc + r, 1), :], sem).start()
        return c
    lax.fori_loop(0, tc, issue, 0, unroll=8)
    pltpu.make_async_copy(y_hbm.at[pl.ds(0, 2 * tc), :], buf, sem).wait()
    gates = gate_ref[...]
    y = gates[:, 0:1] * buf[0:tc, :] + gates[:, 1:2] * buf[tc:2 * tc, :]
    xn = x_ref[...] + y
    xo_ref[...] = xn
    if with_norm:
        ho_ref[...] = _rms(xn, g_ref[...]).astype(BF16)


def moe_combine(x, gates, y, pos_flat, g_next, tc=256):
    t, d = x.shape
    with_norm = g_next is not None
    g_arr = (g_next if with_norm else jnp.ones((d,), F32)).reshape(1, d)
    out_specs = [pl.BlockSpec((tc, d), lambda i, pos: (i, 0))]
    out_shape = [jax.ShapeDtypeStruct((t, d), F32)]
    if with_norm:
        out_specs.append(pl.BlockSpec((tc, d), lambda i, pos: (i, 0)))
        out_shape.append(jax.ShapeDtypeStruct((t, d), BF16))
    outs = pl.pallas_call(
        functools.partial(_combine_kernel, tc=tc, with_norm=with_norm),
        grid_spec=pltpu.PrefetchScalarGridSpec(
            num_scalar_prefetch=1,
            grid=(t // tc,),
            in_specs=[pl.BlockSpec((tc, d), lambda i, pos: (i, 0)),
                      pl.BlockSpec((tc, LANES), lambda i, pos: (i, 0)),
                      pl.BlockSpec(memory_space=pl.ANY),
                      pl.BlockSpec((1, d), lambda i, pos: (0, 0))],
            out_specs=out_specs,
            scratch_shapes=[pltpu.VMEM((2 * tc, d), F32), pltpu.SemaphoreType.DMA(())]),
        out_shape=out_shape,
        compiler_params=_params(16 * tc * d * 4 + 4 * 1024 * 1024, 1),
        name="moe_combine",
    )(pos_flat, x, gates, y, g_arr)
    return outs if with_norm else (outs[0], None)


def moe_layer(x, g_ffn, w_router, w13, w2, layer, g_next):
    t, d = x.shape
    d_ff = w2.shape[1]
    tm = MOE_TM
    n_tiles = (2 * t) // tm + N_EXPERTS
    n_slots = n_tiles * tm

    h, meta, gates, counts = moe_route(x, g_ffn, w_router)
    cnt = counts[0, :N_EXPERTS]
    tiles_per = (cnt + tm - 1) // tm
    tile_end = jnp.cumsum(tiles_per)
    tile_start = tile_end - tiles_per
    row_start = tile_start * tm
    e1, e2, r1, r2 = meta[:, 0], meta[:, 1], meta[:, 2], meta[:, 3]
    pos = jnp.stack([row_start[e1] + r1, row_start[e2] + r2], axis=1).reshape(-1).astype(I32)
    tile_id = jnp.arange(n_tiles, dtype=I32)
    used = tile_end[-1]
    te = jnp.sum(jnp.minimum(tile_id, used - 1)[:, None] >= tile_end[None, :], axis=1).astype(I32)
    tr = jnp.clip(cnt[te] - (tile_id - tile_start[te]) * tm, 0, tm)
    tr = jnp.where(tile_id < used, tr, 0).astype(I32)
    te = te + layer * N_EXPERTS

    xs = moe_dispatch(h, pos, n_slots)

    tn13 = 1024
    (act,) = grouped_matmul(
        xs, [(w13, 0), (w13, d_ff // tn13)], te, tr, tm=tm, sub=MOE_SUB, tn=tn13,
        n_col_tiles=d_ff // tn13, extras=[], out_dtypes=[BF16], out_cols=d_ff,
        epilogue=_epi_swiglu, name="moe_w13")
    tn2 = 512
    (y,) = grouped_matmul(
        act, [(w2, 0)], te, tr, tm=tm, sub=MOE_SUB, tn=tn2, n_col_tiles=d // tn2, extras=[],
        out_dtypes=[F32], out_cols=d, epilogue=_epi_plain, name="moe_w2")
    return moe_combine(x, gates, y, pos, g_next)


def conv_layer(x, h, w_in, w_conv, w_out, layer, seq):
    t, d = x.shape
    tn = 512
    tm = 1024
    te, tr = _dense_tiles(t, tm, layer)
    nt = d // tn
    gb, u = grouped_matmul(
        h, [(w_in, 0), (w_in, nt), (w_in, 2 * nt)], te, tr, tm=tm, sub=tm, tn=tn,
        n_col_tiles=nt, extras=[], out_dtypes=[BF16, BF16], out_cols=d,
        epilogue=_epi_conv_in, name="conv_in")
    z = conv_gate(gb, u, w_conv, seq)
    tn = 1024
    (x,) = grouped_matmul(
        z, [(w_out, 0)], te, tr, tm=tm, sub=tm, tn=tn, n_col_tiles=d // tn,
        extras=[(x, (tm, tn), lambda j, i: (i, j))], out_dtypes=[F32], out_cols=d,
        epilogue=_epi_residual, name="conv_out")
    return x


def dense_ffn(x, h, w13, w2, layer):
    t, d = x.shape
    d_ff = w2.shape[1]
    tn = 1024
    tm = 1024
    te, tr = _dense_tiles(t, tm, layer)
    (act,) = grouped_matmul(
        h, [(w13, 0), (w13, d_ff // tn)], te, tr, tm=tm, sub=512, tn=tn,
        n_col_tiles=d_ff // tn, extras=[], out_dtypes=[BF16], out_cols=d_ff,
        epilogue=_epi_swiglu, name="ffn_w13")
    tn = 512
    tm = 512
    te, tr = _dense_tiles(t, tm, layer)
    (x,) = grouped_matmul(
        act, [(w2, 0)], te, tr, tm=tm, sub=tm, tn=tn, n_col_tiles=d // tn,
        extras=[(x, (tm, tn), lambda j, i: (i, j))], out_dtypes=[F32], out_cols=d,
        epilogue=_epi_residual, name="ffn_w2")
    return x


def attention_layer(x, h, cos, sin, w_qkv, q_gain, k_gain, sink, w_out, layer, batch, seq):
    t, d = x.shape
    n_heads = N_KV_HEADS * Q_PER_KV
    tn = 512
    tm = 1024
    heads_per_tile = tn // HEAD_DIM
    qkv_dim = w_qkv.shape[2]
    n_rope_tiles = (n_heads + N_KV_HEADS) // heads_per_tile
    gain_cols = jnp.concatenate([
        jnp.tile(q_gain.astype(F32) * (1.0 / math.sqrt(HEAD_DIM)), n_heads),
        jnp.tile(k_gain.astype(F32), N_KV_HEADS),
        jnp.ones((N_KV_HEADS * HEAD_DIM,), F32)]).reshape(1, qkv_dim)
    te, tr = _dense_tiles(t, tm, layer)
    (qkv,) = grouped_matmul(
        h, [(w_qkv, 0)], te, tr, tm=tm, sub=tm, tn=tn, n_col_tiles=qkv_dim // tn,
        extras=[(cos, (tm, HEAD_DIM), lambda j, i: (i, 0)),
                (sin, (tm, HEAD_DIM), lambda j, i: (i, 0)),
                (gain_cols, (1, tn), lambda j, i: (0, j))],
        out_dtypes=[BF16], out_cols=qkv_dim,
        epilogue=functools.partial(_epi_qkv, heads_per_tile=heads_per_tile,
                                   n_rope_tiles=n_rope_tiles),
        name="attn_qkv")
    o = window_attention(qkv, sink, batch, seq)
    tn = 1024
    (x,) = grouped_matmul(
        o, [(w_out, 0)], te, tr, tm=tm, sub=tm, tn=tn, n_col_tiles=d // tn,
        extras=[(x, (tm, tn), lambda j, i: (i, j))], out_dtypes=[F32], out_cols=d,
        epilogue=_epi_residual, name="attn_out")
    return x


def kernel(x, positions, norm_mix, norm_ffn, conv_in, conv_w, conv_out, attn_qkv, q_norm, k_norm,
           attn_sink, attn_out, ffn_w13, ffn_w2, router, moe_w13, moe_w2):
    batch, seq, d = x.shape
    depth = norm_mix.shape[0]
    t = batch * seq
    x = x.reshape(t, d)
    moe_w13 = moe_w13.reshape((-1,) + moe_w13.shape[2:])
    moe_w2 = moe_w2.reshape((-1,) + moe_w2.shape[2:])
    cos, sin = rope_tables(positions.reshape(t))
    h = rms_norm_bf16(x, norm_mix[0])
    for i in range(depth):
        j = i // 2
        if i % 2 == 0:
            x = conv_layer(x, h, conv_in, conv_w[j], conv_out, j, seq)
            h = rms_norm_bf16(x, norm_ffn[i])
            x = dense_ffn(x, h, ffn_w13, ffn_w2, j)
            h = rms_norm_bf16(x, norm_mix[i + 1])
        else:
            x = attention_layer(x, h, cos, sin, attn_qkv, q_norm[j], k_norm[j],
                                attn_sink[j], attn_out, j, batch, seq)
            g_next = norm_mix[i + 1] if i + 1 < depth else None
            x, h = moe_layer(x, norm_ffn[i], router[j], moe_w13, moe_w2, j, g_next)
    return x.reshape(batch, seq, d)
```

```python
import functools
import math

import jax
import jax.numpy as jnp
from jax import lax
from jax.experimental import pallas as pl
from jax.experimental.pallas import tpu as pltpu

F32 = jnp.float32
BF16 = jnp.bfloat16
I32 = jnp.int32

HEAD_DIM = 128
N_KV_HEADS = 4
Q_PER_KV = 4
WINDOW = 128
ATTN_BLOCK = 128
ROPE_THETA = 10000.0
N_EXPERTS = 8
RMS_EPS = 1e-6
MASK_VALUE = -1e30

LANES = 128
BF16_SUBLANES = 16
VMEM_BUDGET = 60000 * 1024

MOE_TM = 512
MOE_SUB = 256


def _params(vmem_bytes, n_grid):
    return pltpu.CompilerParams(
        dimension_semantics=("arbitrary",) * n_grid,
        vmem_limit_bytes=int(min(vmem_bytes, VMEM_BUDGET)))


def _nbytes(shape, dtype):
    return math.prod(shape) * jnp.dtype(dtype).itemsize


def _rms(x, g):
    ms = jnp.mean(x * x, axis=-1, keepdims=True)
    return x * lax.rsqrt(ms + RMS_EPS) * g


def _norm_kernel(x_ref, g_ref, o_ref):
    o_ref[...] = _rms(x_ref[...], g_ref[...]).astype(o_ref.dtype)


def rms_norm_bf16(x, g, tm=512):
    t, d = x.shape
    return pl.pallas_call(
        _norm_kernel,
        grid=(t // tm,),
        in_specs=[pl.BlockSpec((tm, d), lambda i: (i, 0)),
                  pl.BlockSpec((1, d), lambda i: (0, 0))],
        out_specs=pl.BlockSpec((tm, d), lambda i: (i, 0)),
        out_shape=jax.ShapeDtypeStruct((t, d), BF16),
        compiler_params=_params(6 * tm * d * 4, 1),
        name="rms_norm",
    )(x, g.reshape(1, d))


def _gmm_kernel(te_ref, tr_ref, nx_ref, ti_ref, x_ref, *refs, n_w, n_extra, n_out, col_offs, tn,
                sub, ragged, epilogue):
    del ti_ref
    w_refs = refs[:n_w]
    e_refs = refs[n_w:n_w + n_extra]
    o_refs = refs[n_w + n_extra:n_w + n_extra + n_out]
    stage, wb, sem = refs[n_w + n_extra + n_out:]
    j = pl.program_id(0)
    i = pl.program_id(1)
    n_j = pl.num_programs(0)
    tm = x_ref.shape[0]
    cur = te_ref[i]
    first_of_group = jnp.logical_or(i == 0, cur != te_ref[jnp.maximum(i - 1, 0)])

    def weight_copy(v, group, jj):
        col = pl.multiple_of((jj + col_offs[v]) * tn, tn)
        return pltpu.make_async_copy(
            w_refs[v].at[group, :, pl.ds(col, tn)], stage.at[v], sem.at[v])

    @pl.when(jnp.logical_and(j == 0, i == 0))
    def _prime():
        for v in range(n_w):
            weight_copy(v, cur, j).start()

    @pl.when(first_of_group)
    def _load_weights():
        for v in range(n_w):
            weight_copy(v, cur, j).wait()
            wb[v] = stage[v].astype(BF16)
        nxt = nx_ref[i]

        @pl.when(nxt >= 0)
        def _():
            for v in range(n_w):
                weight_copy(v, nxt, j).start()

        @pl.when(jnp.logical_and(nxt < 0, j + 1 < n_j))
        def _():
            for v in range(n_w):
                weight_copy(v, te_ref[0], j + 1).start()

    def compute(rows):
        xb = x_ref[rows, :]
        accs = [jnp.dot(xb, wb[v], preferred_element_type=F32) for v in range(n_w)]
        epilogue(j, rows, accs, e_refs, o_refs)

    for s in range(tm // sub):
        rows = slice(s * sub, (s + 1) * sub)
        if not ragged:
            compute(rows)
            continue
        pl.when(tr_ref[i] > s * sub)(functools.partial(compute, rows))

        @pl.when(tr_ref[i] <= s * sub)
        def _zero_fill():
            for o in o_refs:
                o[rows, :] = jnp.zeros((sub, o.shape[1]), o.dtype)


def grouped_matmul(x, weights, te, tr, *, tm, sub, tn, n_col_tiles, extras, out_dtypes,
                   out_cols, epilogue, name, ragged=False):
    rows, k = x.shape
    n_w, n_extra, n_out = len(weights), len(extras), len(out_dtypes)
    later = jnp.where(te[None, :] > te[:, None], te[None, :], jnp.iinfo(jnp.int32).max)
    nx = jnp.min(later, axis=1)
    nx = jnp.where(nx == jnp.iinfo(jnp.int32).max, -1, nx).astype(I32)
    n_used = jnp.sum((tr > 0).astype(I32))
    ti = jnp.minimum(jnp.arange(rows // tm, dtype=I32), n_used - 1)

    in_specs = [pl.BlockSpec((tm, k), lambda j, i, te, tr, nx, ti: (ti[i], 0))]
    in_specs += [pl.BlockSpec(memory_space=pl.ANY) for _ in weights]
    for _, bshape, imap in extras:
        in_specs.append(pl.BlockSpec(
            bshape, lambda j, i, te, tr, nx, ti, imap=imap: imap(j, ti[i])))
    out_specs = [pl.BlockSpec((tm, tn), lambda j, i, *_: (i, j)) for _ in out_dtypes]
    out_shape = [jax.ShapeDtypeStruct((rows, out_cols), dt) for dt in out_dtypes]

    vmem = 2 * _nbytes((tm, k), x.dtype)
    vmem += n_w * (_nbytes((k, tn), F32) + _nbytes((k, tn), BF16))
    vmem += sum(2 * _nbytes(b, a.dtype) for a, b, _ in extras)
    vmem += sum(2 * _nbytes((tm, tn), dt) for dt in out_dtypes)
    vmem += (n_w + 3) * _nbytes((sub, tn), F32)
    vmem += _nbytes((sub, k), x.dtype) + 6 * 1024 * 1024

    return pl.pallas_call(
        functools.partial(_gmm_kernel, n_w=n_w, n_extra=n_extra, n_out=n_out,
                          col_offs=tuple(off for _, off in weights), tn=tn, sub=sub,
                          ragged=ragged, epilogue=epilogue),
        grid_spec=pltpu.PrefetchScalarGridSpec(
            num_scalar_prefetch=4,
            grid=(n_col_tiles, rows // tm),
            in_specs=in_specs,
            out_specs=out_specs,
            scratch_shapes=[pltpu.VMEM((n_w, k, tn), F32),
                            pltpu.VMEM((n_w, k, tn), BF16),
                            pltpu.SemaphoreType.DMA((n_w,))]),
        out_shape=out_shape,
        compiler_params=_params(vmem, 2),
        name=name,
    )(te, tr, nx, ti, x, *[w for w, _ in weights], *[a for a, _, _ in extras])


def _dense_tiles(rows, tm, group):
    n = rows // tm
    return jnp.full((n,), group, I32), jnp.full((n,), tm, I32)


def _epi_swiglu(j, rows, accs, e_refs, o_refs):
    g, u = accs
    o_refs[0][rows, :] = (g * jax.nn.sigmoid(g) * u).astype(BF16)


def _epi_residual(j, rows, accs, e_refs, o_refs):
    o_refs[0][rows, :] = e_refs[0][rows, :] + accs[0]


def _epi_plain(j, rows, accs, e_refs, o_refs):
    o_refs[0][rows, :] = accs[0]


def _epi_conv_in(j, rows, accs, e_refs, o_refs):
    gate_b, gate_c, val = accs
    o_refs[0][rows, :] = gate_b.astype(BF16)
    o_refs[1][rows, :] = (gate_c * val).astype(BF16)


def _epi_qkv(j, rows, accs, e_refs, o_refs, *, heads_per_tile, n_rope_tiles):
    acc = accs[0]
    cos_ref, sin_ref, gain_ref = e_refs
    o = o_refs[0]
    cos = cos_ref[rows, :]
    sin = sin_ref[rows, :]
    is_qk = j < n_rope_tiles
    for hh in range(heads_per_tile):
        sl = slice(hh * HEAD_DIM, (hh + 1) * HEAD_DIM)
        a = acc[:, sl]
        y = _rms(a, gain_ref[:, sl])
        rot = pltpu.roll(y, HEAD_DIM // 2, axis=1)
        o[rows, sl] = jnp.where(is_qk, y * cos + rot * sin, a).astype(BF16)


def _conv_kernel(gb_ref, u_ref, up_ref, un_ref, w_ref, o_ref, *, tm, seq):
    i = pl.program_id(0)
    u = u_ref[...].astype(F32)
    w = w_ref[...]
    row = lax.broadcasted_iota(I32, u.shape, 0)
    starts_seq = (i * tm) % seq == 0
    ends_seq = ((i + 1) * tm) % seq == 0
    prev_row = up_ref[...].astype(F32)[BF16_SUBLANES - 1:BF16_SUBLANES, :]
    next_row = un_ref[...].astype(F32)[0:1, :]
    prev_row = jnp.where(starts_seq, 0.0, prev_row)
    next_row = jnp.where(ends_seq, 0.0, next_row)
    u_prev = jnp.where(row == 0, prev_row, pltpu.roll(u, 1, axis=0))
    u_next = jnp.where(row == tm - 1, next_row, pltpu.roll(u, tm - 1, axis=0))
    conv = w[0:1, :] * u_prev + w[1:2, :] * u + w[2:3, :] * u_next
    o_ref[...] = (gb_ref[...].astype(F32) * conv).astype(BF16)


def conv_gate(gb, u, w_conv, seq, tm=512):
    t, d = u.shape
    hb = tm // BF16_SUBLANES
    n_halo = t // BF16_SUBLANES
    return pl.pallas_call(
        functools.partial(_conv_kernel, tm=tm, seq=seq),
        grid=(t // tm,),
        in_specs=[pl.BlockSpec((tm, d), lambda i: (i, 0)),
                  pl.BlockSpec((tm, d), lambda i: (i, 0)),
                  pl.BlockSpec((BF16_SUBLANES, d), lambda i: (jnp.maximum(i * hb - 1, 0), 0)),
                  pl.BlockSpec((BF16_SUBLANES, d),
                               lambda i: (jnp.minimum((i + 1) * hb, n_halo - 1), 0)),
                  pl.BlockSpec((3, d), lambda i: (0, 0))],
        out_specs=pl.BlockSpec((tm, d), lambda i: (i, 0)),
        out_shape=jax.ShapeDtypeStruct((t, d), BF16),
        compiler_params=_params(12 * tm * d * 4, 1),
        name="conv_gate",
    )(gb, u, u, u, w_conv)


def _rope_kernel(pos_ref, invf_ref, sign_ref, cos_ref, sin_ref):
    ang = pos_ref[...].astype(F32) * invf_ref[...]
    cos_ref[...] = jnp.cos(ang)
    sin_ref[...] = jnp.sin(ang) * sign_ref[...]


def rope_tables(positions, tm=512):
    t = positions.shape[0]
    half = HEAD_DIM // 2
    inv_freq = ROPE_THETA ** (-jnp.arange(0, half, dtype=F32) / half)
    invf = jnp.concatenate([inv_freq, inv_freq]).reshape(1, HEAD_DIM)
    sign = jnp.concatenate([-jnp.ones((half,), F32), jnp.ones((half,), F32)]).reshape(1, HEAD_DIM)
    return pl.pallas_call(
        _rope_kernel,
        grid=(t // tm,),
        in_specs=[pl.BlockSpec((tm, 1), lambda i: (i, 0)),
                  pl.BlockSpec((1, HEAD_DIM), lambda i: (0, 0)),
                  pl.BlockSpec((1, HEAD_DIM), lambda i: (0, 0))],
        out_specs=[pl.BlockSpec((tm, HEAD_DIM), lambda i: (i, 0))] * 2,
        out_shape=[jax.ShapeDtypeStruct((t, HEAD_DIM), F32)] * 2,
        compiler_params=_params(16 * tm * HEAD_DIM * 4, 1),
        name="rope_tables",
    )(positions.reshape(t, 1), invf, sign)


ATTN_ROWS = Q_PER_KV * ATTN_BLOCK
ATTN_KEYS = 3 * ATTN_BLOCK
ATTN_UNROLL = 2


def _attn_kernel(q_ref, k_ref, v_ref, sink_ref, o_ref, bias_ref):
    seq = q_ref.shape[0]
    sink = jnp.broadcast_to(sink_ref[...], (ATTN_ROWS, HEAD_DIM))
    q_off = lax.broadcasted_iota(I32, (ATTN_BLOCK, ATTN_KEYS), 0)
    k_off = lax.broadcasted_iota(I32, (ATTN_BLOCK, ATTN_KEYS), 1)
    for c in range(3):
        inside = jnp.abs(k_off - q_off - c * ATTN_BLOCK) <= WINDOW
        bias_ref[c] = jnp.where(inside, 0.0, MASK_VALUE)
    ones = jnp.ones((ATTN_KEYS, HEAD_DIM), BF16)

    def block(n):
        q0 = pl.multiple_of(n * ATTN_BLOCK, ATTN_BLOCK)
        k0 = pl.multiple_of(jnp.clip((n - 1) * ATTN_BLOCK, 0, seq - ATTN_KEYS), ATTN_BLOCK)
        qb = q_ref[pl.ds(q0, ATTN_BLOCK), :]
        qs = jnp.concatenate(
            [qb[:, g * HEAD_DIM:(g + 1) * HEAD_DIM] for g in range(Q_PER_KV)], axis=0)
        kw = k_ref[pl.ds(k0, ATTN_KEYS), :]
        v_ext = jnp.concatenate([v_ref[pl.ds(k0, ATTN_KEYS), :], ones], axis=1)
        s_all = lax.dot_general(qs, kw, (((1,), (1,)), ((), ())), preferred_element_type=F32)
        bias = bias_ref[(q0 - k0) // ATTN_BLOCK]
        for g in range(Q_PER_KV):
            rows = slice(g * ATTN_BLOCK, (g + 1) * ATTN_BLOCK)
            s = s_all[rows, :] + bias
            sink_g = sink[rows, :]
            m = jnp.maximum(jnp.max(s, axis=-1, keepdims=True), sink_g)
            p = jnp.exp(s - jnp.concatenate([m] * (ATTN_KEYS // HEAD_DIM), axis=1))
            pv = jnp.dot(p.astype(BF16), v_ext, preferred_element_type=F32)
            denom = pv[:, HEAD_DIM:] + jnp.exp(sink_g - m)
            o = pv[:, :HEAD_DIM] / denom
            o_ref[pl.ds(q0, ATTN_BLOCK), g * HEAD_DIM:(g + 1) * HEAD_DIM] = o.astype(BF16)

    def body(n2, carry):
        for r in range(ATTN_UNROLL):
            block(n2 * ATTN_UNROLL + r)
        return carry

    lax.fori_loop(0, seq // (ATTN_BLOCK * ATTN_UNROLL), body, 0)


def window_attention(qkv, sink, batch, seq):
    t = qkv.shape[0]
    n_heads = N_KV_HEADS * Q_PER_KV
    gw = Q_PER_KV * HEAD_DIM
    sink_col = jnp.repeat(sink.astype(F32).reshape(N_KV_HEADS, Q_PER_KV), ATTN_BLOCK, axis=1)
    sink_col = sink_col.reshape(N_KV_HEADS, ATTN_ROWS, 1)
    return pl.pallas_call(
        _attn_kernel,
        grid=(batch, N_KV_HEADS),
        in_specs=[pl.BlockSpec((seq, gw), lambda b, h: (b, h)),
                  pl.BlockSpec((seq, HEAD_DIM), lambda b, h: (b, n_heads + h)),
                  pl.BlockSpec((seq, HEAD_DIM), lambda b, h: (b, n_heads + N_KV_HEADS + h)),
                  pl.BlockSpec((None, ATTN_ROWS, 1), lambda b, h: (h, 0, 0))],
        out_specs=pl.BlockSpec((seq, gw), lambda b, h: (b, h)),
        out_shape=jax.ShapeDtypeStruct((t, n_heads * HEAD_DIM), BF16),
        scratch_shapes=[pltpu.VMEM((3, ATTN_BLOCK, ATTN_KEYS), F32)],
        compiler_params=_params(6 * seq * gw * 2 + 24 * 1024 * 1024, 2),
        name="window_attention",
    )(qkv, qkv, qkv, sink_col)


def _route_kernel(x_ref, g_ref, r_ref, h_ref, meta_ref, gate_ref, cnt_ref, carry_ref, *, tm):
    i = pl.program_id(0)

    @pl.when(i == 0)
    def _init():
        carry_ref[...] = jnp.zeros(carry_ref.shape, carry_ref.dtype)

    h = _rms(x_ref[...], g_ref[...])
    h_ref[...] = h.astype(BF16)
    logits = jnp.dot(h, r_ref[...], preferred_element_type=F32,
                     precision=lax.Precision.HIGHEST)
    lane = lax.broadcasted_iota(I32, logits.shape, 1)
    lane_f = lane.astype(F32)
    neg_inf = jnp.float32(-jnp.inf)
    logits = jnp.where(lane < N_EXPERTS, logits, neg_inf)
    v1 = jnp.max(logits, axis=-1, keepdims=True)
    i1 = jnp.min(jnp.where(logits == v1, lane_f, float(LANES)), axis=-1, keepdims=True)
    rest = jnp.where(lane_f == i1, neg_inf, logits)
    v2 = jnp.max(rest, axis=-1, keepdims=True)
    i2 = jnp.min(jnp.where(rest == v2, lane_f, float(LANES)), axis=-1, keepdims=True)
    e21 = jnp.exp(v2 - v1)
    g1 = 1.0 / (1.0 + e21)
    g2 = e21 / (1.0 + e21)

    sel1 = lane_f == i1
    sel2 = lane_f == i2
    onehot = jnp.logical_or(sel1, sel2)
    r_io = lax.broadcasted_iota(I32, (tm, tm), 0)
    c_io = lax.broadcasted_iota(I32, (tm, tm), 1)
    strict_lower = (c_io < r_io).astype(BF16)
    carry = carry_ref[...]
    before = jnp.dot(strict_lower, onehot.astype(BF16), preferred_element_type=F32) + carry
    rank1 = jnp.sum(jnp.where(sel1, before, 0.0), axis=-1, keepdims=True)
    rank2 = jnp.sum(jnp.where(sel2, before, 0.0), axis=-1, keepdims=True)
    carry = carry + jnp.sum(onehot.astype(F32), axis=0, keepdims=True)
    carry_ref[...] = carry
    cnt_ref[...] = carry.astype(I32)

    meta = jnp.where(lane == 0, i1, jnp.where(lane == 1, i2,
           jnp.where(lane == 2, rank1, jnp.where(lane == 3, rank2, 0.0))))
    meta_ref[...] = meta.astype(I32)
    gate_ref[...] = jnp.where(lane == 0, g1, jnp.where(lane == 1, g2, 0.0))


def moe_route(x, g, w_router, tm=512):
    t, d = x.shape
    r_pad = jnp.zeros((d, LANES), F32).at[:, :N_EXPERTS].set(w_router)
    return pl.pallas_call(
        functools.partial(_route_kernel, tm=tm),
        grid=(t // tm,),
        in_specs=[pl.BlockSpec((tm, d), lambda i: (i, 0)),
                  pl.BlockSpec((1, d), lambda i: (0, 0)),
                  pl.BlockSpec((d, LANES), lambda i: (0, 0))],
        out_specs=[pl.BlockSpec((tm, d), lambda i: (i, 0)),
                   pl.BlockSpec((tm, LANES), lambda i: (i, 0)),
                   pl.BlockSpec((tm, LANES), lambda i: (i, 0)),
                   pl.BlockSpec((1, LANES), lambda i: (0, 0))],
        out_shape=[jax.ShapeDtypeStruct((t, d), BF16),
                   jax.ShapeDtypeStruct((t, LANES), I32),
                   jax.ShapeDtypeStruct((t, LANES), F32),
                   jax.ShapeDtypeStruct((1, LANES), I32)],
        scratch_shapes=[pltpu.VMEM((1, LANES), F32)],
        compiler_params=_params(10 * tm * d * 4 + 8 * 1024 * 1024, 1),
        name="moe_route",
    )(x, g.reshape(1, d), r_pad)


DISPATCH_TM = 512
DISPATCH_UNROLL = 8


def _dispatch_kernel(pos_ref, h_ref, zeros_hbm, xs_hbm, sem):
    del zeros_hbm
    tm = h_ref.shape[0]
    base = pl.program_id(0) * tm

    def body(r, c):
        for k in range(2):
            pltpu.make_async_copy(
                h_ref.at[r], xs_hbm.at[pos_ref[2 * (base + r) + k]], sem).start(priority=k)
        return c
    lax.fori_loop(0, tm, body, 0, unroll=DISPATCH_UNROLL)
    for k in range(2):
        pltpu.make_async_copy(h_ref, xs_hbm.at[pl.ds(0, tm)], sem).wait()


def moe_dispatch(h, pos_flat, n_slots):
    t, d = h.shape
    h3 = h.reshape(t, d // LANES, LANES)
    zeros = jnp.zeros((n_slots, d // LANES, LANES), BF16)
    tm = DISPATCH_TM
    xs = pl.pallas_call(
        _dispatch_kernel,
        grid_spec=pltpu.PrefetchScalarGridSpec(
            num_scalar_prefetch=1,
            grid=(t // tm,),
            in_specs=[pl.BlockSpec((tm, d // LANES, LANES), lambda i, pos: (i, 0, 0)),
                      pl.BlockSpec(memory_space=pl.ANY)],
            out_specs=pl.BlockSpec(memory_space=pl.ANY),
            scratch_shapes=[pltpu.SemaphoreType.DMA(())]),
        out_shape=jax.ShapeDtypeStruct(zeros.shape, BF16),
        input_output_aliases={2: 0},
        compiler_params=_params(4 * tm * d * 2 + 4 * 1024 * 1024, 1),
        name="moe_dispatch",
    )(pos_flat, h3, zeros)
    return xs.reshape(n_slots, d)


def _combine_kernel(pos_ref, x_ref, gate_ref, y_hbm, g_ref, *refs, tc, with_norm):
    if with_norm:
        xo_ref, ho_ref, buf, sem = refs
    else:
        xo_ref, buf, sem = refs
    base = pl.program_id(0) * tc

    def issue(r, c):
        p1 = pos_ref[2 * (base + r)]
        p2 = pos_ref[2 * (base + r) + 1]
        pltpu.make_async_copy(
            y_hbm.at[pl.ds(p1, 1), :], buf.at[pl.ds(r, 1), :], sem).start(priority=0)
        pltpu.make_async_copy(
            y_hbm.at[pl.ds(p2, 1), :], buf.at[pl.ds(tc + r, 1), :], sem).start(priority=1)
        return c
    lax.fori_loop(0, tc, issue, 0, unroll=8)
    pltpu.make_async_copy(y_hbm.at[pl.ds(0, 2 * tc), :], buf, sem).wait()
    gates = gate_ref[...]
    y = gates[:, 0:1] * buf[0:tc, :] + gates[:, 1:2] * buf[tc:2 * tc, :]
    xn = x_ref[...] + y
    xo_ref[...] = xn
    if with_norm:
        ho_ref[...] = _rms(xn, g_ref[...]).astype(BF16)


def moe_combine(x, gates, y, pos_flat, g_next, tc=256):
    t, d = x.shape
    with_norm = g_next is not None
    g_arr = (g_next if with_norm else jnp.ones((d,), F32)).reshape(1, d)
    out_specs = [pl.BlockSpec((tc, d), lambda i, pos: (i, 0))]
    out_shape = [jax.ShapeDtypeStruct((t, d), F32)]
    if with_norm:
        out_specs.append(pl.BlockSpec((tc, d), lambda i, pos: (i, 0)))
        out_shape.append(jax.ShapeDtypeStruct((t, d), BF16))
    outs = pl.pallas_call(
        functools.partial(_combine_kernel, tc=tc, with_norm=with_norm),
        grid_spec=pltpu.PrefetchScalarGridSpec(
            num_scalar_prefetch=1,
            grid=(t // tc,),
            in_specs=[pl.BlockSpec((tc, d), lambda i, pos: (i, 0)),
                      pl.BlockSpec((tc, LANES), lambda i, pos: (i, 0)),
                      pl.BlockSpec(memory_space=pl.ANY),
                      pl.BlockSpec((1, d), lambda i, pos: (0, 0))],
            out_specs=out_specs,
            scratch_shapes=[pltpu.VMEM((2 * tc, d), F32), pltpu.SemaphoreType.DMA(())]),
        out_shape=out_shape,
        compiler_params=_params(16 * tc * d * 4 + 4 * 1024 * 1024, 1),
        name="moe_combine",
    )(pos_flat, x, gates, y, g_arr)
    return outs if with_norm else (outs[0], None)


def moe_layer(x, g_ffn, w_router, w13, w2, layer, g_next):
    t, d = x.shape
    d_ff = w2.shape[1]
    tm = MOE_TM
    n_tiles = (2 * t) // tm + N_EXPERTS
    n_slots = n_tiles * tm

    h, meta, gates, counts = moe_route(x, g_ffn, w_router)
    cnt = counts[0, :N_EXPERTS]
    tiles_per = (cnt + tm - 1) // tm
    tile_end = jnp.cumsum(tiles_per)
    tile_start = tile_end - tiles_per
    row_start = tile_start * tm
    e1, e2, r1, r2 = meta[:, 0], meta[:, 1], meta[:, 2], meta[:, 3]
    pos = jnp.stack([row_start[e1] + r1, row_start[e2] + r2], axis=1).reshape(-1).astype(I32)
    tile_id = jnp.arange(n_tiles, dtype=I32)
    used = tile_end[-1]
    te = jnp.sum(jnp.minimum(tile_id, used - 1)[:, None] >= tile_end[None, :], axis=1).astype(I32)
    tr = jnp.clip(cnt[te] - (tile_id - tile_start[te]) * tm, 0, tm)
    tr = jnp.where(tile_id < used, tr, 0).astype(I32)
    te = te + layer * N_EXPERTS

    xs = moe_dispatch(h, pos, n_slots)

    tn13 = 1024
    (act,) = grouped_matmul(
        xs, [(w13, 0), (w13, d_ff // tn13)], te, tr, tm=tm, sub=MOE_SUB, tn=tn13,
        n_col_tiles=d_ff // tn13, extras=[], out_dtypes=[BF16], out_cols=d_ff,
        epilogue=_epi_swiglu, name="moe_w13", ragged=True)
    tn2 = 512
    (y,) = grouped_matmul(
        act, [(w2, 0)], te, tr, tm=tm, sub=MOE_SUB, tn=tn2, n_col_tiles=d // tn2, extras=[],
        out_dtypes=[F32], out_cols=d, epilogue=_epi_plain, name="moe_w2", ragged=True)
    return moe_combine(x, gates, y, pos, g_next)


def conv_layer(x, h, w_in, w_conv, w_out, layer, seq):
    t, d = x.shape
    tn = 512
    tm = 1024
    te, tr = _dense_tiles(t, tm, layer)
    nt = d // tn
    gb, u = grouped_matmul(
        h, [(w_in, 0), (w_in, nt), (w_in, 2 * nt)], te, tr, tm=tm, sub=tm, tn=tn,
        n_col_tiles=nt, extras=[], out_dtypes=[BF16, BF16], out_cols=d,
        epilogue=_epi_conv_in, name="conv_in")
    z = conv_gate(gb, u, w_conv, seq)
    tn = 1024
    (x,) = grouped_matmul(
        z, [(w_out, 0)], te, tr, tm=tm, sub=tm, tn=tn, n_col_tiles=d // tn,
        extras=[(x, (tm, tn), lambda j, i: (i, j))], out_dtypes=[F32], out_cols=d,
        epilogue=_epi_residual, name="conv_out")
    return x


def dense_ffn(x, h, w13, w2, layer):
    t, d = x.shape
    d_ff = w2.shape[1]
    tn = 1024
    tm = 1024
    te, tr = _dense_tiles(t, tm, layer)
    (act,) = grouped_matmul(
        h, [(w13, 0), (w13, d_ff // tn)], te, tr, tm=tm, sub=512, tn=tn,
        n_col_tiles=d_ff // tn, extras=[], out_dtypes=[BF16], out_cols=d_ff,
        epilogue=_epi_swiglu, name="ffn_w13")
    tn = 512
    tm = 512
    te, tr = _dense_tiles(t, tm, layer)
    (x,) = grouped_matmul(
        act, [(w2, 0)], te, tr, tm=tm, sub=tm, tn=tn, n_col_tiles=d // tn,
        extras=[(x, (tm, tn), lambda j, i: (i, j))], out_dtypes=[F32], out_cols=d,
        epilogue=_epi_residual, name="ffn_w2")
    return x


def attention_layer(x, h, cos, sin, w_qkv, q_gain, k_gain, sink, w_out, layer, batch, seq):
    t, d = x.shape
    n_heads = N_KV_HEADS * Q_PER_KV
    tn = 512
    tm = 1024
    heads_per_tile = tn // HEAD_DIM
    qkv_dim = w_qkv.shape[2]
    n_rope_tiles = (n_heads + N_KV_HEADS) // heads_per_tile
    gain_cols = jnp.concatenate([
        jnp.tile(q_gain.astype(F32) * (1.0 / math.sqrt(HEAD_DIM)), n_heads),
        jnp.tile(k_gain.astype(F32), N_KV_HEADS),
        jnp.ones((N_KV_HEADS * HEAD_DIM,), F32)]).reshape(1, qkv_dim)
    te, tr = _dense_tiles(t, tm, layer)
    (qkv,) = grouped_matmul(
        h, [(w_qkv, 0)], te, tr, tm=tm, sub=256, tn=tn, n_col_tiles=qkv_dim // tn,
        extras=[(cos, (tm, HEAD_DIM), lambda j, i: (i, 0)),
                (sin, (tm, HEAD_DIM), lambda j, i: (i, 0)),
                (gain_cols, (1, tn), lambda j, i: (0, j))],
        out_dtypes=[BF16], out_cols=qkv_dim,
        epilogue=functools.partial(_epi_qkv, heads_per_tile=heads_per_tile,
                                   n_rope_tiles=n_rope_tiles),
        name="attn_qkv")
    o = window_attention(qkv, sink, batch, seq)
    tn = 1024
    (x,) = grouped_matmul(
        o, [(w_out, 0)], te, tr, tm=tm, sub=tm, tn=tn, n_col_tiles=d // tn,
        extras=[(x, (tm, tn), lambda j, i: (i, j))], out_dtypes=[F32], out_cols=d,
        epilogue=_epi_residual, name="attn_out")
    return x


def kernel(x, positions, norm_mix, norm_ffn, conv_in, conv_w, conv_out, attn_qkv, q_norm, k_norm,
           attn_sink, attn_out, ffn_w13, ffn_w2, router, moe_w13, moe_w2):
    batch, seq, d = x.shape
    depth = norm_mix.shape[0]
    t = batch * seq
    x = x.reshape(t, d)
    moe_w13 = moe_w13.reshape((-1,) + moe_w13.shape[2:])
    moe_w2 = moe_w2.reshape((-1,) + moe_w2.shape[2:])
    cos, sin = rope_tables(positions.reshape(t))
    h = rms_norm_bf16(x, norm_mix[0])
    for i in range(depth):
        j = i // 2
        if i % 2 == 0:
            x = conv_layer(x, h, conv_in, conv_w[j], conv_out, j, seq)
            h = rms_norm_bf16(x, norm_ffn[i])
            x = dense_ffn(x, h, ffn_w13, ffn_w2, j)
            h = rms_norm_bf16(x, norm_mix[i + 1])
        else:
            x = attention_layer(x, h, cos, sin, attn_qkv, q_norm[j], k_norm[j],
                                attn_sink[j], attn_out, j, batch, seq)
            g_next = norm_mix[i + 1] if i + 1 < depth else None
            x, h = moe_layer(x, norm_ffn[i], router[j], moe_w13, moe_w2, j, g_next)
    return x.reshape(batch, seq, d)
```

```python
import functools
import math

import jax
import jax.numpy as jnp
from jax import lax
from jax.experimental import pallas as pl
from jax.experimental.pallas import tpu as pltpu

F32 = jnp.float32
BF16 = jnp.bfloat16
I32 = jnp.int32

HEAD_DIM = 128
N_KV_HEADS = 4
Q_PER_KV = 4
WINDOW = 128
ATTN_BLOCK = 128
ROPE_THETA = 10000.0
N_EXPERTS = 8
RMS_EPS = 1e-6
MASK_VALUE = -1e30

LANES = 128
BF16_SUBLANES = 16
VMEM_BUDGET = 60000 * 1024

MOE_TM = 1024
MOE_SUB = 256


def _params(vmem_bytes, n_grid):
    return pltpu.CompilerParams(
        dimension_semantics=("arbitrary",) * n_grid,
        vmem_limit_bytes=int(min(vmem_bytes, VMEM_BUDGET)))


def _nbytes(shape, dtype):
    return math.prod(shape) * jnp.dtype(dtype).itemsize


def _rms(x, g):
    ms = jnp.mean(x * x, axis=-1, keepdims=True)
    return x * lax.rsqrt(ms + RMS_EPS) * g


def _norm_kernel(x_ref, g_ref, o_ref):
    o_ref[...] = _rms(x_ref[...], g_ref[...]).astype(o_ref.dtype)


def rms_norm_bf16(x, g, tm=512):
    t, d = x.shape
    return pl.pallas_call(
        _norm_kernel,
        grid=(t // tm,),
        in_specs=[pl.BlockSpec((tm, d), lambda i: (i, 0)),
                  pl.BlockSpec((1, d), lambda i: (0, 0))],
        out_specs=pl.BlockSpec((tm, d), lambda i: (i, 0)),
        out_shape=jax.ShapeDtypeStruct((t, d), BF16),
        compiler_params=_params(6 * tm * d * 4, 1),
        name="rms_norm",
    )(x, g.reshape(1, d))


def _gmm_kernel(te_ref, tr_ref, nx_ref, ti_ref, x_ref, *refs, n_w, n_extra, n_out, col_offs, tn,
                sub, ragged, epilogue):
    del ti_ref
    w_refs = refs[:n_w]
    e_refs = refs[n_w:n_w + n_extra]
    o_refs = refs[n_w + n_extra:n_w + n_extra + n_out]
    stage, wb, sem = refs[n_w + n_extra + n_out:]
    j = pl.program_id(0)
    i = pl.program_id(1)
    n_j = pl.num_programs(0)
    tm = x_ref.shape[0]
    cur = te_ref[i]
    first_of_group = jnp.logical_or(i == 0, cur != te_ref[jnp.maximum(i - 1, 0)])

    def weight_copy(v, group, jj):
        col = pl.multiple_of((jj + col_offs[v]) * tn, tn)
        return pltpu.make_async_copy(
            w_refs[v].at[group, :, pl.ds(col, tn)], stage.at[v], sem.at[v])

    @pl.when(jnp.logical_and(j == 0, i == 0))
    def _prime():
        for v in range(n_w):
            weight_copy(v, cur, j).start()

    @pl.when(first_of_group)
    def _load_weights():
        for v in range(n_w):
            weight_copy(v, cur, j).wait()
            wb[v] = stage[v].astype(BF16)
        nxt = nx_ref[i]

        @pl.when(nxt >= 0)
        def _():
            for v in range(n_w):
                weight_copy(v, nxt, j).start()

        @pl.when(jnp.logical_and(nxt < 0, j + 1 < n_j))
        def _():
            for v in range(n_w):
                weight_copy(v, te_ref[0], j + 1).start()

    def compute(rows):
        xb = x_ref[rows, :]
        accs = [jnp.dot(xb, wb[v], preferred_element_type=F32) for v in range(n_w)]
        epilogue(j, rows, accs, e_refs, o_refs)

    n_sub = tm // sub
    sub_rows = [slice(s * sub, (s + 1) * sub) for s in range(n_sub)]

    def tile_body(n_live):
        for s in range(n_live):
            compute(sub_rows[s])
        for s in range(n_live, n_sub):
            for o in o_refs:
                o[sub_rows[s], :] = jnp.zeros((sub, o.shape[1]), o.dtype)

    if not ragged:
        tile_body(n_sub)
    else:
        live = (tr_ref[i] + (sub - 1)) // sub
        for n_live in range(n_sub + 1):
            pl.when(live == n_live)(functools.partial(tile_body, n_live))


def grouped_matmul(x, weights, te, tr, *, tm, sub, tn, n_col_tiles, extras, out_dtypes,
                   out_cols, epilogue, name, ragged=False):
    rows, k = x.shape
    n_w, n_extra, n_out = len(weights), len(extras), len(out_dtypes)
    later = jnp.where(te[None, :] > te[:, None], te[None, :], jnp.iinfo(jnp.int32).max)
    nx = jnp.min(later, axis=1)
    nx = jnp.where(nx == jnp.iinfo(jnp.int32).max, -1, nx).astype(I32)
    tile_id = jnp.arange(rows // tm, dtype=I32)
    has_rows_before = jnp.logical_and(tile_id[None, :] <= tile_id[:, None], tr[None, :] > 0)
    ti = jnp.max(jnp.where(has_rows_before, tile_id[None, :], 0), axis=1).astype(I32)

    in_specs = [pl.BlockSpec((tm, k), lambda j, i, te, tr, nx, ti: (ti[i], 0))]
    in_specs += [pl.BlockSpec(memory_space=pl.ANY) for _ in weights]
    for _, bshape, imap in extras:
        in_specs.append(pl.BlockSpec(
            bshape, lambda j, i, te, tr, nx, ti, imap=imap: imap(j, ti[i])))
    out_specs = [pl.BlockSpec((tm, tn), lambda j, i, *_: (i, j)) for _ in out_dtypes]
    out_shape = [jax.ShapeDtypeStruct((rows, out_cols), dt) for dt in out_dtypes]

    vmem = 2 * _nbytes((tm, k), x.dtype)
    vmem += n_w * (_nbytes((k, tn), F32) + _nbytes((k, tn), BF16))
    vmem += sum(2 * _nbytes(b, a.dtype) for a, b, _ in extras)
    vmem += sum(2 * _nbytes((tm, tn), dt) for dt in out_dtypes)
    vmem += (n_w + 3) * _nbytes((sub, tn), F32)
    vmem += _nbytes((sub, k), x.dtype) + 6 * 1024 * 1024

    return pl.pallas_call(
        functools.partial(_gmm_kernel, n_w=n_w, n_extra=n_extra, n_out=n_out,
                          col_offs=tuple(off for _, off in weights), tn=tn, sub=sub,
                          ragged=ragged, epilogue=epilogue),
        grid_spec=pltpu.PrefetchScalarGridSpec(
            num_scalar_prefetch=4,
            grid=(n_col_tiles, rows // tm),
            in_specs=in_specs,
            out_specs=out_specs,
            scratch_shapes=[pltpu.VMEM((n_w, k, tn), F32),
                            pltpu.VMEM((n_w, k, tn), BF16),
                            pltpu.SemaphoreType.DMA((n_w,))]),
        out_shape=out_shape,
        compiler_params=_params(vmem, 2),
        name=name,
    )(te, tr, nx, ti, x, *[w for w, _ in weights], *[a for a, _, _ in extras])


def _dense_tiles(rows, tm, group):
    n = rows // tm
    return jnp.full((n,), group, I32), jnp.full((n,), tm, I32)


def _epi_swiglu(j, rows, accs, e_refs, o_refs):
    g, u = accs
    o_refs[0][rows, :] = (g * jax.nn.sigmoid(g) * u).astype(BF16)


def _epi_residual(j, rows, accs, e_refs, o_refs):
    o_refs[0][rows, :] = e_refs[0][rows, :] + accs[0]


def _epi_plain(j, rows, accs, e_refs, o_refs):
    o_refs[0][rows, :] = accs[0]


def _epi_conv_in(j, rows, accs, e_refs, o_refs):
    gate_b, gate_c, val = accs
    o_refs[0][rows, :] = gate_b.astype(BF16)
    o_refs[1][rows, :] = (gate_c * val).astype(BF16)


def _epi_qkv(j, rows, accs, e_refs, o_refs, *, heads_per_tile, n_rope_tiles):
    acc = accs[0]
    cos_ref, sin_ref, gain_ref = e_refs
    o = o_refs[0]
    cos = cos_ref[rows, :]
    sin = sin_ref[rows, :]
    is_qk = j < n_rope_tiles
    for hh in range(heads_per_tile):
        sl = slice(hh * HEAD_DIM, (hh + 1) * HEAD_DIM)
        a = acc[:, sl]
        y = _rms(a, gain_ref[:, sl])
        rot = pltpu.roll(y, HEAD_DIM // 2, axis=1)
        o[rows, sl] = jnp.where(is_qk, y * cos + rot * sin, a).astype(BF16)


def _conv_kernel(gb_ref, u_ref, up_ref, un_ref, w_ref, o_ref, *, tm, seq):
    i = pl.program_id(0)
    u = u_ref[...].astype(F32)
    w = w_ref[...]
    row = lax.broadcasted_iota(I32, u.shape, 0)
    starts_seq = (i * tm) % seq == 0
    ends_seq = ((i + 1) * tm) % seq == 0
    prev_row = up_ref[...].astype(F32)[BF16_SUBLANES - 1:BF16_SUBLANES, :]
    next_row = un_ref[...].astype(F32)[0:1, :]
    prev_row = jnp.where(starts_seq, 0.0, prev_row)
    next_row = jnp.where(ends_seq, 0.0, next_row)
    u_prev = jnp.where(row == 0, prev_row, pltpu.roll(u, 1, axis=0))
    u_next = jnp.where(row == tm - 1, next_row, pltpu.roll(u, tm - 1, axis=0))
    conv = w[0:1, :] * u_prev + w[1:2, :] * u + w[2:3, :] * u_next
    o_ref[...] = (gb_ref[...].astype(F32) * conv).astype(BF16)


def conv_gate(gb, u, w_conv, seq, tm=512):
    t, d = u.shape
    hb = tm // BF16_SUBLANES
    n_halo = t // BF16_SUBLANES
    return pl.pallas_call(
        functools.partial(_conv_kernel, tm=tm, seq=seq),
        grid=(t // tm,),
        in_specs=[pl.BlockSpec((tm, d), lambda i: (i, 0)),
                  pl.BlockSpec((tm, d), lambda i: (i, 0)),
                  pl.BlockSpec((BF16_SUBLANES, d), lambda i: (jnp.maximum(i * hb - 1, 0), 0)),
                  pl.BlockSpec((BF16_SUBLANES, d),
                               lambda i: (jnp.minimum((i + 1) * hb, n_halo - 1), 0)),
                  pl.BlockSpec((3, d), lambda i: (0, 0))],
        out_specs=pl.BlockSpec((tm, d), lambda i: (i, 0)),
        out_shape=jax.ShapeDtypeStruct((t, d), BF16),
        compiler_params=_params(12 * tm * d * 4, 1),
        name="conv_gate",
    )(gb, u, u, u, w_conv)


def _rope_kernel(pos_ref, invf_ref, sign_ref, cos_ref, sin_ref):
    ang = pos_ref[...].astype(F32) * invf_ref[...]
    cos_ref[...] = jnp.cos(ang)
    sin_ref[...] = jnp.sin(ang) * sign_ref[...]


def rope_tables(positions, tm=512):
    t = positions.shape[0]
    half = HEAD_DIM // 2
    inv_freq = ROPE_THETA ** (-jnp.arange(0, half, dtype=F32) / half)
    invf = jnp.concatenate([inv_freq, inv_freq]).reshape(1, HEAD_DIM)
    sign = jnp.concatenate([-jnp.ones((half,), F32), jnp.ones((half,), F32)]).reshape(1, HEAD_DIM)
    return pl.pallas_call(
        _rope_kernel,
        grid=(t // tm,),
        in_specs=[pl.BlockSpec((tm, 1), lambda i: (i, 0)),
                  pl.BlockSpec((1, HEAD_DIM), lambda i: (0, 0)),
                  pl.BlockSpec((1, HEAD_DIM), lambda i: (0, 0))],
        out_specs=[pl.BlockSpec((tm, HEAD_DIM), lambda i: (i, 0))] * 2,
        out_shape=[jax.ShapeDtypeStruct((t, HEAD_DIM), F32)] * 2,
        compiler_params=_params(16 * tm * HEAD_DIM * 4, 1),
        name="rope_tables",
    )(positions.reshape(t, 1), invf, sign)


ATTN_ROWS = Q_PER_KV * ATTN_BLOCK
ATTN_KEYS = 3 * ATTN_BLOCK
ATTN_UNROLL = 2


def _attn_kernel(q_ref, k_ref, v_ref, sink_ref, o_ref, bias_ref):
    seq = q_ref.shape[0]
    sink = jnp.broadcast_to(sink_ref[...], (ATTN_ROWS, HEAD_DIM))
    q_off = lax.broadcasted_iota(I32, (ATTN_BLOCK, ATTN_KEYS), 0)
    k_off = lax.broadcasted_iota(I32, (ATTN_BLOCK, ATTN_KEYS), 1)
    for c in range(3):
        inside = jnp.abs(k_off - q_off - c * ATTN_BLOCK) <= WINDOW
        bias_ref[c] = jnp.where(inside, 0.0, MASK_VALUE)
    ones = jnp.ones((ATTN_KEYS, HEAD_DIM), BF16)

    def block(n):
        q0 = pl.multiple_of(n * ATTN_BLOCK, ATTN_BLOCK)
        k0 = pl.multiple_of(jnp.clip((n - 1) * ATTN_BLOCK, 0, seq - ATTN_KEYS), ATTN_BLOCK)
        qb = q_ref[pl.ds(q0, ATTN_BLOCK), :]
        qs = jnp.concatenate(
            [qb[:, g * HEAD_DIM:(g + 1) * HEAD_DIM] for g in range(Q_PER_KV)], axis=0)
        kw = k_ref[pl.ds(k0, ATTN_KEYS), :]
        v_ext = jnp.concatenate([v_ref[pl.ds(k0, ATTN_KEYS), :], ones], axis=1)
        s_all = lax.dot_general(qs, kw, (((1,), (1,)), ((), ())), preferred_element_type=F32)
        bias = bias_ref[(q0 - k0) // ATTN_BLOCK]
        for g in range(Q_PER_KV):
            rows = slice(g * ATTN_BLOCK, (g + 1) * ATTN_BLOCK)
            s = s_all[rows, :] + bias
            sink_g = sink[rows, :]
            m = jnp.maximum(jnp.max(s, axis=-1, keepdims=True), sink_g)
            p = jnp.exp(s - jnp.concatenate([m] * (ATTN_KEYS // HEAD_DIM), axis=1))
            pv = jnp.dot(p.astype(BF16), v_ext, preferred_element_type=F32)
            denom = pv[:, HEAD_DIM:] + jnp.exp(sink_g - m)
            o = pv[:, :HEAD_DIM] / denom
            o_ref[pl.ds(q0, ATTN_BLOCK), g * HEAD_DIM:(g + 1) * HEAD_DIM] = o.astype(BF16)

    def body(n2, carry):
        for r in range(ATTN_UNROLL):
            block(n2 * ATTN_UNROLL + r)
        return carry

    lax.fori_loop(0, seq // (ATTN_BLOCK * ATTN_UNROLL), body, 0)


def window_attention(qkv, sink, batch, seq):
    t = qkv.shape[0]
    n_heads = N_KV_HEADS * Q_PER_KV
    gw = Q_PER_KV * HEAD_DIM
    sink_col = jnp.repeat(sink.astype(F32).reshape(N_KV_HEADS, Q_PER_KV), ATTN_BLOCK, axis=1)
    sink_col = sink_col.reshape(N_KV_HEADS, ATTN_ROWS, 1)
    return pl.pallas_call(
        _attn_kernel,
        grid=(batch, N_KV_HEADS),
        in_specs=[pl.BlockSpec((seq, gw), lambda b, h: (b, h)),
                  pl.BlockSpec((seq, HEAD_DIM), lambda b, h: (b, n_heads + h)),
                  pl.BlockSpec((seq, HEAD_DIM), lambda b, h: (b, n_heads + N_KV_HEADS + h)),
                  pl.BlockSpec((None, ATTN_ROWS, 1), lambda b, h: (h, 0, 0))],
        out_specs=pl.BlockSpec((seq, gw), lambda b, h: (b, h)),
        out_shape=jax.ShapeDtypeStruct((t, n_heads * HEAD_DIM), BF16),
        scratch_shapes=[pltpu.VMEM((3, ATTN_BLOCK, ATTN_KEYS), F32)],
        compiler_params=_params(6 * seq * gw * 2 + 24 * 1024 * 1024, 2),
        name="window_attention",
    )(qkv, qkv, qkv, sink_col)


def _route_kernel(x_ref, g_ref, r_ref, h_ref, meta_ref, gate_ref, cnt_ref, carry_ref, *, tm):
    i = pl.program_id(0)

    @pl.when(i == 0)
    def _init():
        carry_ref[...] = jnp.zeros(carry_ref.shape, carry_ref.dtype)

    h = _rms(x_ref[...], g_ref[...])
    h_ref[...] = h.astype(BF16)
    logits = jnp.dot(h, r_ref[...], preferred_element_type=F32,
                     precision=lax.Precision.HIGHEST)
    lane = lax.broadcasted_iota(I32, logits.shape, 1)
    lane_f = lane.astype(F32)
    neg_inf = jnp.float32(-jnp.inf)
    logits = jnp.where(lane < N_EXPERTS, logits, neg_inf)
    v1 = jnp.max(logits, axis=-1, keepdims=True)
    i1 = jnp.min(jnp.where(logits == v1, lane_f, float(LANES)), axis=-1, keepdims=True)
    rest = jnp.where(lane_f == i1, neg_inf, logits)
    v2 = jnp.max(rest, axis=-1, keepdims=True)
    i2 = jnp.min(jnp.where(rest == v2, lane_f, float(LANES)), axis=-1, keepdims=True)
    e21 = jnp.exp(v2 - v1)
    g1 = 1.0 / (1.0 + e21)
    g2 = e21 / (1.0 + e21)

    sel1 = lane_f == i1
    sel2 = lane_f == i2
    onehot = jnp.logical_or(sel1, sel2)
    r_io = lax.broadcasted_iota(I32, (tm, tm), 0)
    c_io = lax.broadcasted_iota(I32, (tm, tm), 1)
    strict_lower = (c_io < r_io).astype(BF16)
    carry = carry_ref[...]
    before = jnp.dot(strict_lower, onehot.astype(BF16), preferred_element_type=F32) + carry
    rank1 = jnp.sum(jnp.where(sel1, before, 0.0), axis=-1, keepdims=True)
    rank2 = jnp.sum(jnp.where(sel2, before, 0.0), axis=-1, keepdims=True)
    carry = carry + jnp.sum(onehot.astype(F32), axis=0, keepdims=True)
    carry_ref[...] = carry
    cnt_ref[...] = carry.astype(I32)

    meta = jnp.where(lane == 0, i1, jnp.where(lane == 1, i2,
           jnp.where(lane == 2, rank1, jnp.where(lane == 3, rank2, 0.0))))
    meta_ref[...] = meta.astype(I32)
    gate_ref[...] = jnp.where(lane == 0, g1, jnp.where(lane == 1, g2, 0.0))


def moe_route(x, g, w_router, tm=512):
    t, d = x.shape
    r_pad = jnp.zeros((d, LANES), F32).at[:, :N_EXPERTS].set(w_router)
    return pl.pallas_call(
        functools.partial(_route_kernel, tm=tm),
        grid=(t // tm,),
        in_specs=[pl.BlockSpec((tm, d), lambda i: (i, 0)),
                  pl.BlockSpec((1, d), lambda i: (0, 0)),
                  pl.BlockSpec((d, LANES), lambda i: (0, 0))],
        out_specs=[pl.BlockSpec((tm, d), lambda i: (i, 0)),
                   pl.BlockSpec((tm, LANES), lambda i: (i, 0)),
                   pl.BlockSpec((tm, LANES), lambda i: (i, 0)),
                   pl.BlockSpec((1, LANES), lambda i: (0, 0))],
        out_shape=[jax.ShapeDtypeStruct((t, d), BF16),
                   jax.ShapeDtypeStruct((t, LANES), I32),
                   jax.ShapeDtypeStruct((t, LANES), F32),
                   jax.ShapeDtypeStruct((1, LANES), I32)],
        scratch_shapes=[pltpu.VMEM((1, LANES), F32)],
        compiler_params=_params(10 * tm * d * 4 + 8 * 1024 * 1024, 1),
        name="moe_route",
    )(x, g.reshape(1, d), r_pad)


DISPATCH_TM = 512
DISPATCH_UNROLL = 8


def _dispatch_kernel(pos_ref, h_ref, zeros_hbm, xs_hbm, sem):
    del zeros_hbm
    tm = h_ref.shape[0]
    base = pl.program_id(0) * tm

    def body(r, c):
        for k in range(2):
            pltpu.make_async_copy(
                h_ref.at[r], xs_hbm.at[pos_ref[2 * (base + r) + k]], sem).start(priority=k)
        return c
    lax.fori_loop(0, tm, body, 0, unroll=DISPATCH_UNROLL)
    for k in range(2):
        pltpu.make_async_copy(h_ref, xs_hbm.at[pl.ds(0, tm)], sem).wait()


def moe_dispatch(h, pos_flat, n_slots):
    t, d = h.shape
    h3 = h.reshape(t, d // LANES, LANES)
    zeros = jnp.zeros((n_slots, d // LANES, LANES), BF16)
    tm = DISPATCH_TM
    xs = pl.pallas_call(
        _dispatch_kernel,
        grid_spec=pltpu.PrefetchScalarGridSpec(
            num_scalar_prefetch=1,
            grid=(t // tm,),
            in_specs=[pl.BlockSpec((tm, d // LANES, LANES), lambda i, pos: (i, 0, 0)),
                      pl.BlockSpec(memory_space=pl.ANY)],
            out_specs=pl.BlockSpec(memory_space=pl.ANY),
            scratch_shapes=[pltpu.SemaphoreType.DMA(())]),
        out_shape=jax.ShapeDtypeStruct(zeros.shape, BF16),
        input_output_aliases={2: 0},
        compiler_params=_params(4 * tm * d * 2 + 4 * 1024 * 1024, 1),
        name="moe_dispatch",
    )(pos_flat, h3, zeros)
    return xs.reshape(n_slots, d)


def _combine_kernel(pos_ref, x_ref, gate_ref, y_hbm, g_ref, *refs, tc, with_norm):
    if with_norm:
        xo_ref, ho_ref, buf, sem = refs
    else:
        xo_ref, buf, sem = refs
    i = pl.program_id(0)

    def issue(tile, slot):
        def body(r, c):
            t = tile * tc + r
            for k in range(2):
                pltpu.make_async_copy(
                    y_hbm.at[pl.ds(pos_ref[2 * t + k], 1), :],
                    buf.at[slot, pl.ds(k * tc + r, 1), :], sem.at[slot]).start(priority=k)
            return c
        lax.fori_loop(0, tc, body, 0, unroll=8)

    @pl.when(i == 0)
    def _prime():
        issue(0, 0)

    @pl.when(i + 1 < pl.num_programs(0))
    def _prefetch():
        issue(i + 1, (i + 1) % 2)

    slot = i % 2
    pltpu.make_async_copy(y_hbm.at[pl.ds(0, 2 * tc), :], buf.at[slot], sem.at[slot]).wait()
    gates = gate_ref[...]
    y = gates[:, 0:1] * buf[slot, 0:tc, :] + gates[:, 1:2] * buf[slot, tc:2 * tc, :]
    xn = x_ref[...] + y
    xo_ref[...] = xn
    if with_norm:
        ho_ref[...] = _rms(xn, g_ref[...]).astype(BF16)


def moe_combine(x, gates, y, pos_flat, g_next, tc=256):
    t, d = x.shape
    with_norm = g_next is not None
    g_arr = (g_next if with_norm else jnp.ones((d,), F32)).reshape(1, d)
    out_specs = [pl.BlockSpec((tc, d), lambda i, pos: (i, 0))]
    out_shape = [jax.ShapeDtypeStruct((t, d), F32)]
    if with_norm:
        out_specs.append(pl.BlockSpec((tc, d), lambda i, pos: (i, 0)))
        out_shape.append(jax.ShapeDtypeStruct((t, d), BF16))
    outs = pl.pallas_call(
        functools.partial(_combine_kernel, tc=tc, with_norm=with_norm),
        grid_spec=pltpu.PrefetchScalarGridSpec(
            num_scalar_prefetch=1,
            grid=(t // tc,),
            in_specs=[pl.BlockSpec((tc, d), lambda i, pos: (i, 0)),
                      pl.BlockSpec((tc, LANES), lambda i, pos: (i, 0)),
                      pl.BlockSpec(memory_space=pl.ANY),
                      pl.BlockSpec((1, d), lambda i, pos: (0, 0))],
            out_specs=out_specs,
            scratch_shapes=[pltpu.VMEM((2, 2 * tc, d), F32), pltpu.SemaphoreType.DMA((2,))]),
        out_shape=out_shape,
        compiler_params=_params(20 * tc * d * 4 + 4 * 1024 * 1024, 1),
        name="moe_combine",
    )(pos_flat, x, gates, y, g_arr)
    return outs if with_norm else (outs[0], None)


def moe_layer(x, g_ffn, w_router, w13, w2, layer, g_next):
    t, d = x.shape
    d_ff = w2.shape[1]
    tm = MOE_TM
    n_tiles = (2 * t) // tm + N_EXPERTS
    n_slots = n_tiles * tm

    h, meta, gates, counts = moe_route(x, g_ffn, w_router)
    cnt = counts[0, :N_EXPERTS]
    tiles_per = (cnt + tm - 1) // tm
    tile_end = jnp.cumsum(tiles_per)
    tile_start = tile_end - tiles_per
    row_start = tile_start * tm
    e1, e2, r1, r2 = meta[:, 0], meta[:, 1], meta[:, 2], meta[:, 3]
    pos = jnp.stack([row_start[e1] + r1, row_start[e2] + r2], axis=1).reshape(-1).astype(I32)
    tile_id = jnp.arange(n_tiles, dtype=I32)
    used = tile_end[-1]
    te = jnp.sum(jnp.minimum(tile_id, used - 1)[:, None] >= tile_end[None, :], axis=1).astype(I32)
    tr = jnp.clip(cnt[te] - (tile_id - tile_start[te]) * tm, 0, tm)
    tr = jnp.where(tile_id < used, tr, 0).astype(I32)
    te = te + layer * N_EXPERTS

    xs = moe_dispatch(h, pos, n_slots)

    tn13 = 1024
    (act,) = grouped_matmul(
        xs, [(w13, 0), (w13, d_ff // tn13)], te, tr, tm=tm, sub=MOE_SUB, tn=tn13,
        n_col_tiles=d_ff // tn13, extras=[], out_dtypes=[BF16], out_cols=d_ff,
        epilogue=_epi_swiglu, name="moe_w13", ragged=True)
    tn2 = 512
    half = tm // 2
    te2 = jnp.repeat(te, 2)
    tr2 = jnp.stack([jnp.minimum(tr, half), jnp.maximum(tr - half, 0)], axis=1).reshape(-1)
    (y,) = grouped_matmul(
        act, [(w2, 0)], te2, tr2, tm=half, sub=MOE_SUB, tn=tn2, n_col_tiles=d // tn2, extras=[],
        out_dtypes=[F32], out_cols=d, epilogue=_epi_plain, name="moe_w2", ragged=True)
    return moe_combine(x, gates, y, pos, g_next)


def conv_layer(x, h, w_in, w_conv, w_out, layer, seq):
    t, d = x.shape
    tn = 512
    tm = 1024
    te, tr = _dense_tiles(t, tm, layer)
    nt = d // tn
    gb, u = grouped_matmul(
        h, [(w_in, 0), (w_in, nt), (w_in, 2 * nt)], te, tr, tm=tm, sub=tm // 2, tn=tn,
        n_col_tiles=nt, extras=[], out_dtypes=[BF16, BF16], out_cols=d,
        epilogue=_epi_conv_in, name="conv_in")
    z = conv_gate(gb, u, w_conv, seq)
    tn = 1024
    (x,) = grouped_matmul(
        z, [(w_out, 0)], te, tr, tm=tm, sub=tm // 2, tn=tn, n_col_tiles=d // tn,
        extras=[(x, (tm, tn), lambda j, i: (i, j))], out_dtypes=[F32], out_cols=d,
        epilogue=_epi_residual, name="conv_out")
    return x


def dense_ffn(x, h, w13, w2, layer):
    t, d = x.shape
    d_ff = w2.shape[1]
    tn = 1024
    tm = 1024
    te, tr = _dense_tiles(t, tm, layer)
    (act,) = grouped_matmul(
        h, [(w13, 0), (w13, d_ff // tn)], te, tr, tm=tm, sub=512, tn=tn,
        n_col_tiles=d_ff // tn, extras=[], out_dtypes=[BF16], out_cols=d_ff,
        epilogue=_epi_swiglu, name="ffn_w13")
    tn = 512
    tm = 512
    te, tr = _dense_tiles(t, tm, layer)
    (x,) = grouped_matmul(
        act, [(w2, 0)], te, tr, tm=tm, sub=tm // 2, tn=tn, n_col_tiles=d // tn,
        extras=[(x, (tm, tn), lambda j, i: (i, j))], out_dtypes=[F32], out_cols=d,
        epilogue=_epi_residual, name="ffn_w2")
    return x


def attention_layer(x, h, cos, sin, w_qkv, q_gain, k_gain, sink, w_out, layer, batch, seq):
    t, d = x.shape
    n_heads = N_KV_HEADS * Q_PER_KV
    tn = 512
    tm = 1024
    heads_per_tile = tn // HEAD_DIM
    qkv_dim = w_qkv.shape[2]
    n_rope_tiles = (n_heads + N_KV_HEADS) // heads_per_tile
    gain_cols = jnp.concatenate([
        jnp.tile(q_gain.astype(F32) * (1.0 / math.sqrt(HEAD_DIM)), n_heads),
        jnp.tile(k_gain.astype(F32), N_KV_HEADS),
        jnp.ones((N_KV_HEADS * HEAD_DIM,), F32)]).reshape(1, qkv_dim)
    te, tr = _dense_tiles(t, tm, layer)
    (qkv,) = grouped_matmul(
        h, [(w_qkv, 0)], te, tr, tm=tm, sub=256, tn=tn, n_col_tiles=qkv_dim // tn,
        extras=[(cos, (tm, HEAD_DIM), lambda j, i: (i, 0)),
                (sin, (tm, HEAD_DIM), lambda j, i: (i, 0)),
                (gain_cols, (1, tn), lambda j, i: (0, j))],
        out_dtypes=[BF16], out_cols=qkv_dim,
        epilogue=functools.partial(_epi_qkv, heads_per_tile=heads_per_tile,
                                   n_rope_tiles=n_rope_tiles),
        name="attn_qkv")
    o = window_attention(qkv, sink, batch, seq)
    tn = 1024
    (x,) = grouped_matmul(
        o, [(w_out, 0)], te, tr, tm=tm, sub=tm // 2, tn=tn, n_col_tiles=d // tn,
        extras=[(x, (tm, tn), lambda j, i: (i, j))], out_dtypes=[F32], out_cols=d,
        epilogue=_epi_residual, name="attn_out")
    return x


def kernel(x, positions, norm_mix, norm_ffn, conv_in, conv_w, conv_out, attn_qkv, q_norm, k_norm,
           attn_sink, attn_out, ffn_w13, ffn_w2, router, moe_w13, moe_w2):
    batch, seq, d = x.shape
    depth = norm_mix.shape[0]
    t = batch * seq
    x = x.reshape(t, d)
    moe_w13 = moe_w13.reshape((-1,) + moe_w13.shape[2:])
    moe_w2 = moe_w2.reshape((-1,) + moe_w2.shape[2:])
    cos, sin = rope_tables(positions.reshape(t))
    h = rms_norm_bf16(x, norm_mix[0])
    for i in range(depth):
        j = i // 2
        if i % 2 == 0:
            x = conv_layer(x, h, conv_in, conv_w[j], conv_out, j, seq)
            h = rms_norm_bf16(x, norm_ffn[i])
            x = dense_ffn(x, h, ffn_w13, ffn_w2, j)
            h = rms_norm_bf16(x, norm_mix[i + 1])
        else:
            x = attention_layer(x, h, cos, sin, attn_qkv, q_norm[j], k_norm[j],
                                attn_sink[j], attn_out, j, batch, seq)
            g_next = norm_mix[i + 1] if i + 1 < depth else None
            x, h = moe_layer(x, norm_ffn[i], router[j], moe_w13, moe_w2, j, g_next)
    return x.reshape(batch, seq, d)
```

```python
import functools
import math

import jax
import jax.numpy as jnp
from jax import lax
from jax.experimental import pallas as pl
from jax.experimental.pallas import tpu as pltpu

F32 = jnp.float32
BF16 = jnp.bfloat16
I32 = jnp.int32

HEAD_DIM = 128
N_KV_HEADS = 4
Q_PER_KV = 4
WINDOW = 128
ATTN_BLOCK = 128
ROPE_THETA = 10000.0
N_EXPERTS = 8
RMS_EPS = 1e-6
MASK_VALUE = -1e30

LANES = 128
BF16_SUBLANES = 16
VMEM_BUDGET = 60000 * 1024

MOE_TM = 1024
MOE_SUB = 128


def _params(vmem_bytes, n_grid):
    return pltpu.CompilerParams(
        dimension_semantics=("arbitrary",) * n_grid,
        vmem_limit_bytes=int(min(vmem_bytes, VMEM_BUDGET)))


def _nbytes(shape, dtype):
    return math.prod(shape) * jnp.dtype(dtype).itemsize


def _rms(x, g):
    ms = jnp.mean(x * x, axis=-1, keepdims=True)
    return x * lax.rsqrt(ms + RMS_EPS) * g


def _norm_kernel(x_ref, g_ref, o_ref):
    o_ref[...] = _rms(x_ref[...], g_ref[...]).astype(o_ref.dtype)


def rms_norm_bf16(x, g, tm=512):
    t, d = x.shape
    return pl.pallas_call(
        _norm_kernel,
        grid=(t // tm,),
        in_specs=[pl.BlockSpec((tm, d), lambda i: (i, 0)),
                  pl.BlockSpec((1, d), lambda i: (0, 0))],
        out_specs=pl.BlockSpec((tm, d), lambda i: (i, 0)),
        out_shape=jax.ShapeDtypeStruct((t, d), BF16),
        compiler_params=_params(6 * tm * d * 4, 1),
        name="rms_norm",
    )(x, g.reshape(1, d))


def _gmm_kernel(te_ref, tr_ref, nx_ref, ti_ref, x_ref, *refs, n_w, n_extra, n_out, col_offs, tn,
                sub, ragged, epilogue):
    del ti_ref
    w_refs = refs[:n_w]
    e_refs = refs[n_w:n_w + n_extra]
    o_refs = refs[n_w + n_extra:n_w + n_extra + n_out]
    stage, wb, sem = refs[n_w + n_extra + n_out:]
    j = pl.program_id(0)
    i = pl.program_id(1)
    n_j = pl.num_programs(0)
    tm = x_ref.shape[0]
    cur = te_ref[i]
    first_of_group = jnp.logical_or(i == 0, cur != te_ref[jnp.maximum(i - 1, 0)])

    def weight_copy(v, group, jj):
        col = pl.multiple_of((jj + col_offs[v]) * tn, tn)
        return pltpu.make_async_copy(
            w_refs[v].at[group, :, pl.ds(col, tn)], stage.at[v], sem.at[v])

    @pl.when(jnp.logical_and(j == 0, i == 0))
    def _prime():
        for v in range(n_w):
            weight_copy(v, cur, j).start()

    @pl.when(first_of_group)
    def _load_weights():
        for v in range(n_w):
            weight_copy(v, cur, j).wait()
            wb[v] = stage[v].astype(BF16)
        nxt = nx_ref[i]

        @pl.when(nxt >= 0)
        def _():
            for v in range(n_w):
                weight_copy(v, nxt, j).start()

        @pl.when(jnp.logical_and(nxt < 0, j + 1 < n_j))
        def _():
            for v in range(n_w):
                weight_copy(v, te_ref[0], j + 1).start()

    def compute(rows):
        xb = x_ref[rows, :]
        accs = [jnp.dot(xb, wb[v], preferred_element_type=F32) for v in range(n_w)]
        epilogue(j, rows, accs, e_refs, o_refs)

    n_sub = tm // sub
    sub_rows = [slice(s * sub, (s + 1) * sub) for s in range(n_sub)]

    def tile_body(n_live):
        for s in range(n_live):
            compute(sub_rows[s])
        for s in range(n_live, n_sub):
            for o in o_refs:
                o[sub_rows[s], :] = jnp.zeros((sub, o.shape[1]), o.dtype)

    if not ragged:
        tile_body(n_sub)
    else:
        live = (tr_ref[i] + (sub - 1)) // sub
        for n_live in range(n_sub + 1):
            pl.when(live == n_live)(functools.partial(tile_body, n_live))


def grouped_matmul(x, weights, te, tr, *, tm, sub, tn, n_col_tiles, extras, out_dtypes,
                   out_cols, epilogue, name, ragged=False):
    rows, k = x.shape
    n_w, n_extra, n_out = len(weights), len(extras), len(out_dtypes)
    later = jnp.where(te[None, :] > te[:, None], te[None, :], jnp.iinfo(jnp.int32).max)
    nx = jnp.min(later, axis=1)
    nx = jnp.where(nx == jnp.iinfo(jnp.int32).max, -1, nx).astype(I32)
    tile_id = jnp.arange(rows // tm, dtype=I32)
    has_rows_before = jnp.logical_and(tile_id[None, :] <= tile_id[:, None], tr[None, :] > 0)
    ti = jnp.max(jnp.where(has_rows_before, tile_id[None, :], 0), axis=1).astype(I32)

    in_specs = [pl.BlockSpec((tm, k), lambda j, i, te, tr, nx, ti: (ti[i], 0))]
    in_specs += [pl.BlockSpec(memory_space=pl.ANY) for _ in weights]
    for _, bshape, imap in extras:
        in_specs.append(pl.BlockSpec(
            bshape, lambda j, i, te, tr, nx, ti, imap=imap: imap(j, ti[i])))
    out_specs = [pl.BlockSpec((tm, tn), lambda j, i, *_: (i, j)) for _ in out_dtypes]
    out_shape = [jax.ShapeDtypeStruct((rows, out_cols), dt) for dt in out_dtypes]

    vmem = 2 * _nbytes((tm, k), x.dtype)
    vmem += n_w * (_nbytes((k, tn), F32) + _nbytes((k, tn), BF16))
    vmem += sum(2 * _nbytes(b, a.dtype) for a, b, _ in extras)
    vmem += sum(2 * _nbytes((tm, tn), dt) for dt in out_dtypes)
    vmem += (n_w + 3) * _nbytes((tm, tn), F32)
    vmem += _nbytes((tm, k), x.dtype) + 4 * 1024 * 1024

    return pl.pallas_call(
        functools.partial(_gmm_kernel, n_w=n_w, n_extra=n_extra, n_out=n_out,
                          col_offs=tuple(off for _, off in weights), tn=tn, sub=sub,
                          ragged=ragged, epilogue=epilogue),
        grid_spec=pltpu.PrefetchScalarGridSpec(
            num_scalar_prefetch=4,
            grid=(n_col_tiles, rows // tm),
            in_specs=in_specs,
            out_specs=out_specs,
            scratch_shapes=[pltpu.VMEM((n_w, k, tn), F32),
                            pltpu.VMEM((n_w, k, tn), BF16),
                            pltpu.SemaphoreType.DMA((n_w,))]),
        out_shape=out_shape,
        compiler_params=_params(vmem, 2),
        name=name,
    )(te, tr, nx, ti, x, *[w for w, _ in weights], *[a for a, _, _ in extras])


def _dense_tiles(rows, tm, group):
    n = rows // tm
    return jnp.full((n,), group, I32), jnp.full((n,), tm, I32)


def _epi_swiglu(j, rows, accs, e_refs, o_refs):
    g, u = accs
    o_refs[0][rows, :] = (g * jax.nn.sigmoid(g) * u).astype(BF16)


def _epi_residual(j, rows, accs, e_refs, o_refs):
    o_refs[0][rows, :] = e_refs[0][rows, :] + accs[0]


def _epi_plain(j, rows, accs, e_refs, o_refs):
    o_refs[0][rows, :] = accs[0]


def _epi_conv_in(j, rows, accs, e_refs, o_refs):
    gate_b, gate_c, val = accs
    o_refs[0][rows, :] = gate_b.astype(BF16)
    o_refs[1][rows, :] = (gate_c * val).astype(BF16)


def _epi_qkv(j, rows, accs, e_refs, o_refs, *, heads_per_tile, n_rope_tiles):
    acc = accs[0]
    cos_ref, sin_ref, gain_ref = e_refs
    o = o_refs[0]
    cos = cos_ref[rows, :]
    sin = sin_ref[rows, :]
    is_qk = j < n_rope_tiles
    for hh in range(heads_per_tile):
        sl = slice(hh * HEAD_DIM, (hh + 1) * HEAD_DIM)
        a = acc[:, sl]
        y = _rms(a, gain_ref[:, sl])
        rot = pltpu.roll(y, HEAD_DIM // 2, axis=1)
        o[rows, sl] = jnp.where(is_qk, y * cos + rot * sin, a).astype(BF16)


def _conv_kernel(gb_ref, u_ref, up_ref, un_ref, w_ref, o_ref, *, tm, seq):
    i = pl.program_id(0)
    u = u_ref[...].astype(F32)
    w = w_ref[...]
    row = lax.broadcasted_iota(I32, u.shape, 0)
    starts_seq = (i * tm) % seq == 0
    ends_seq = ((i + 1) * tm) % seq == 0
    prev_row = up_ref[...].astype(F32)[BF16_SUBLANES - 1:BF16_SUBLANES, :]
    next_row = un_ref[...].astype(F32)[0:1, :]
    prev_row = jnp.where(starts_seq, 0.0, prev_row)
    next_row = jnp.where(ends_seq, 0.0, next_row)
    u_prev = jnp.where(row == 0, prev_row, pltpu.roll(u, 1, axis=0))
    u_next = jnp.where(row == tm - 1, next_row, pltpu.roll(u, tm - 1, axis=0))
    conv = w[0:1, :] * u_prev + w[1:2, :] * u + w[2:3, :] * u_next
    o_ref[...] = (gb_ref[...].astype(F32) * conv).astype(BF16)


def conv_gate(gb, u, w_conv, seq, tm=512):
    t, d = u.shape
    hb = tm // BF16_SUBLANES
    n_halo = t // BF16_SUBLANES
    return pl.pallas_call(
        functools.partial(_conv_kernel, tm=tm, seq=seq),
        grid=(t // tm,),
        in_specs=[pl.BlockSpec((tm, d), lambda i: (i, 0)),
                  pl.BlockSpec((tm, d), lambda i: (i, 0)),
                  pl.BlockSpec((BF16_SUBLANES, d), lambda i: (jnp.maximum(i * hb - 1, 0), 0)),
                  pl.BlockSpec((BF16_SUBLANES, d),
                               lambda i: (jnp.minimum((i + 1) * hb, n_halo - 1), 0)),
                  pl.BlockSpec((3, d), lambda i: (0, 0))],
        out_specs=pl.BlockSpec((tm, d), lambda i: (i, 0)),
        out_shape=jax.ShapeDtypeStruct((t, d), BF16),
        compiler_params=_params(12 * tm * d * 4, 1),
        name="conv_gate",
    )(gb, u, u, u, w_conv)


def _rope_kernel(pos_ref, invf_ref, sign_ref, cos_ref, sin_ref):
    ang = pos_ref[...].astype(F32) * invf_ref[...]
    cos_ref[...] = jnp.cos(ang)
    sin_ref[...] = jnp.sin(ang) * sign_ref[...]


def rope_tables(positions, tm=512):
    t = positions.shape[0]
    half = HEAD_DIM // 2
    inv_freq = ROPE_THETA ** (-jnp.arange(0, half, dtype=F32) / half)
    invf = jnp.concatenate([inv_freq, inv_freq]).reshape(1, HEAD_DIM)
    sign = jnp.concatenate([-jnp.ones((half,), F32), jnp.ones((half,), F32)]).reshape(1, HEAD_DIM)
    return pl.pallas_call(
        _rope_kernel,
        grid=(t // tm,),
        in_specs=[pl.BlockSpec((tm, 1), lambda i: (i, 0)),
                  pl.BlockSpec((1, HEAD_DIM), lambda i: (0, 0)),
                  pl.BlockSpec((1, HEAD_DIM), lambda i: (0, 0))],
        out_specs=[pl.BlockSpec((tm, HEAD_DIM), lambda i: (i, 0))] * 2,
        out_shape=[jax.ShapeDtypeStruct((t, HEAD_DIM), F32)] * 2,
        compiler_params=_params(16 * tm * HEAD_DIM * 4, 1),
        name="rope_tables",
    )(positions.reshape(t, 1), invf, sign)


ATTN_ROWS = Q_PER_KV * ATTN_BLOCK
ATTN_KEYS = 3 * ATTN_BLOCK
ATTN_UNROLL = 4


def _attn_kernel(q_ref, k_ref, v_ref, sink_ref, o_ref, bias_ref):
    seq = q_ref.shape[0]
    sink = jnp.broadcast_to(sink_ref[...], (ATTN_ROWS, HEAD_DIM))
    q_off = lax.broadcasted_iota(I32, (ATTN_BLOCK, ATTN_KEYS), 0)
    k_off = lax.broadcasted_iota(I32, (ATTN_BLOCK, ATTN_KEYS), 1)
    for c in range(3):
        inside = jnp.abs(k_off - q_off - c * ATTN_BLOCK) <= WINDOW
        bias_ref[c] = jnp.where(inside, 0.0, MASK_VALUE)
    ones = jnp.ones((ATTN_KEYS, HEAD_DIM), BF16)

    def block(n):
        q0 = pl.multiple_of(n * ATTN_BLOCK, ATTN_BLOCK)
        k0 = pl.multiple_of(jnp.clip((n - 1) * ATTN_BLOCK, 0, seq - ATTN_KEYS), ATTN_BLOCK)
        qb = q_ref[pl.ds(q0, ATTN_BLOCK), :]
        qs = jnp.concatenate(
            [qb[:, g * HEAD_DIM:(g + 1) * HEAD_DIM] for g in range(Q_PER_KV)], axis=0)
        kw = k_ref[pl.ds(k0, ATTN_KEYS), :]
        v_ext = jnp.concatenate([v_ref[pl.ds(k0, ATTN_KEYS), :], ones], axis=1)
        s_all = lax.dot_general(qs, kw, (((1,), (1,)), ((), ())), preferred_element_type=F32)
        bias = bias_ref[(q0 - k0) // ATTN_BLOCK]
        for g in range(Q_PER_KV):
            rows = slice(g * ATTN_BLOCK, (g + 1) * ATTN_BLOCK)
            s = s_all[rows, :] + bias
            sink_g = sink[rows, :]
            m = jnp.maximum(jnp.max(s, axis=-1, keepdims=True), sink_g)
            p = jnp.exp(s - jnp.concatenate([m] * (ATTN_KEYS // HEAD_DIM), axis=1))
            pv = jnp.dot(p.astype(BF16), v_ext, preferred_element_type=F32)
            denom = pv[:, HEAD_DIM:] + jnp.exp(sink_g - m)
            o = pv[:, :HEAD_DIM] / denom
            o_ref[pl.ds(q0, ATTN_BLOCK), g * HEAD_DIM:(g + 1) * HEAD_DIM] = o.astype(BF16)

    def body(n2, carry):
        for r in range(ATTN_UNROLL):
            block(n2 * ATTN_UNROLL + r)
        return carry

    lax.fori_loop(0, seq // (ATTN_BLOCK * ATTN_UNROLL), body, 0)


def window_attention(qkv, sink, batch, seq):
    t = qkv.shape[0]
    n_heads = N_KV_HEADS * Q_PER_KV
    gw = Q_PER_KV * HEAD_DIM
    sink_col = jnp.repeat(sink.astype(F32).reshape(N_KV_HEADS, Q_PER_KV), ATTN_BLOCK, axis=1)
    sink_col = sink_col.reshape(N_KV_HEADS, ATTN_ROWS, 1)
    return pl.pallas_call(
        _attn_kernel,
        grid=(batch, N_KV_HEADS),
        in_specs=[pl.BlockSpec((seq, gw), lambda b, h: (b, h)),
                  pl.BlockSpec((seq, HEAD_DIM), lambda b, h: (b, n_heads + h)),
                  pl.BlockSpec((seq, HEAD_DIM), lambda b, h: (b, n_heads + N_KV_HEADS + h)),
                  pl.BlockSpec((None, ATTN_ROWS, 1), lambda b, h: (h, 0, 0))],
        out_specs=pl.BlockSpec((seq, gw), lambda b, h: (b, h)),
        out_shape=jax.ShapeDtypeStruct((t, n_heads * HEAD_DIM), BF16),
        scratch_shapes=[pltpu.VMEM((3, ATTN_BLOCK, ATTN_KEYS), F32)],
        compiler_params=_params(6 * seq * gw * 2 + 24 * 1024 * 1024, 2),
        name="window_attention",
    )(qkv, qkv, qkv, sink_col)


def _route_kernel(x_ref, g_ref, r_ref, h_ref, meta_ref, gate_ref, cnt_ref, carry_ref, *, tm):
    i = pl.program_id(0)

    @pl.when(i == 0)
    def _init():
        carry_ref[...] = jnp.zeros(carry_ref.shape, carry_ref.dtype)

    h = _rms(x_ref[...], g_ref[...])
    h_ref[...] = h.astype(BF16)
    r = r_ref[...]
    h_hi = h.astype(BF16)
    h_lo = (h - h_hi.astype(F32)).astype(BF16)
    r_hi = r.astype(BF16)
    r_lo = (r - r_hi.astype(F32)).astype(BF16)
    logits = (jnp.dot(h_hi, r_hi, preferred_element_type=F32)
              + (jnp.dot(h_hi, r_lo, preferred_element_type=F32)
                 + jnp.dot(h_lo, r_hi, preferred_element_type=F32)))
    lane = lax.broadcasted_iota(I32, logits.shape, 1)
    lane_f = lane.astype(F32)
    neg_inf = jnp.float32(-jnp.inf)
    logits = jnp.where(lane < N_EXPERTS, logits, neg_inf)
    v1 = jnp.max(logits, axis=-1, keepdims=True)
    i1 = jnp.min(jnp.where(logits == v1, lane_f, float(LANES)), axis=-1, keepdims=True)
    rest = jnp.where(lane_f == i1, neg_inf, logits)
    v2 = jnp.max(rest, axis=-1, keepdims=True)
    i2 = jnp.min(jnp.where(rest == v2, lane_f, float(LANES)), axis=-1, keepdims=True)
    e21 = jnp.exp(v2 - v1)
    g1 = 1.0 / (1.0 + e21)
    g2 = e21 / (1.0 + e21)

    sel1 = lane_f == i1
    sel2 = lane_f == i2
    onehot = jnp.logical_or(sel1, sel2)
    r_io = lax.broadcasted_iota(I32, (tm, tm), 0)
    c_io = lax.broadcasted_iota(I32, (tm, tm), 1)
    strict_lower = (c_io < r_io).astype(BF16)
    carry = carry_ref[...]
    before = jnp.dot(strict_lower, onehot.astype(BF16), preferred_element_type=F32) + carry
    rank1 = jnp.sum(jnp.where(sel1, before, 0.0), axis=-1, keepdims=True)
    rank2 = jnp.sum(jnp.where(sel2, before, 0.0), axis=-1, keepdims=True)
    carry = carry + jnp.sum(onehot.astype(F32), axis=0, keepdims=True)
    carry_ref[...] = carry
    cnt_ref[...] = carry.astype(I32)

    meta = jnp.where(lane == 0, i1, jnp.where(lane == 1, i2,
           jnp.where(lane == 2, rank1, jnp.where(lane == 3, rank2, 0.0))))
    meta_ref[...] = meta.astype(I32)
    gate_ref[...] = jnp.where(lane == 0, g1, jnp.where(lane == 1, g2, 0.0))


def moe_route(x, g, w_router, tm=512):
    t, d = x.shape
    r_pad = jnp.zeros((d, LANES), F32).at[:, :N_EXPERTS].set(w_router)
    return pl.pallas_call(
        functools.partial(_route_kernel, tm=tm),
        grid=(t // tm,),
        in_specs=[pl.BlockSpec((tm, d), lambda i: (i, 0)),
                  pl.BlockSpec((1, d), lambda i: (0, 0)),
                  pl.BlockSpec((d, LANES), lambda i: (0, 0))],
        out_specs=[pl.BlockSpec((tm, d), lambda i: (i, 0)),
                   pl.BlockSpec((tm, LANES), lambda i: (i, 0)),
                   pl.BlockSpec((tm, LANES), lambda i: (i, 0)),
                   pl.BlockSpec((1, LANES), lambda i: (0, 0))],
        out_shape=[jax.ShapeDtypeStruct((t, d), BF16),
                   jax.ShapeDtypeStruct((t, LANES), I32),
                   jax.ShapeDtypeStruct((t, LANES), F32),
                   jax.ShapeDtypeStruct((1, LANES), I32)],
        scratch_shapes=[pltpu.VMEM((1, LANES), F32)],
        compiler_params=_params(10 * tm * d * 4 + 8 * 1024 * 1024, 1),
        name="moe_route",
    )(x, g.reshape(1, d), r_pad)


DISPATCH_TM = 512
DISPATCH_UNROLL = 8


def _dispatch_kernel(pos_ref, h_ref, zeros_hbm, xs_hbm, sem):
    del zeros_hbm
    tm = h_ref.shape[0]
    base = pl.program_id(0) * tm

    def body(r, c):
        for k in range(2):
            pltpu.make_async_copy(
                h_ref.at[r], xs_hbm.at[pos_ref[2 * (base + r) + k]], sem).start(priority=k)
        return c
    lax.fori_loop(0, tm, body, 0, unroll=DISPATCH_UNROLL)
    for k in range(2):
        pltpu.make_async_copy(h_ref, xs_hbm.at[pl.ds(0, tm)], sem).wait()


def moe_dispatch(h, pos_flat, n_slots):
    t, d = h.shape
    h3 = h.reshape(t, d // LANES, LANES)
    zeros = jnp.zeros((n_slots, d // LANES, LANES), BF16)
    tm = DISPATCH_TM
    xs = pl.pallas_call(
        _dispatch_kernel,
        grid_spec=pltpu.PrefetchScalarGridSpec(
            num_scalar_prefetch=1,
            grid=(t // tm,),
            in_specs=[pl.BlockSpec((tm, d // LANES, LANES), lambda i, pos: (i, 0, 0)),
                      pl.BlockSpec(memory_space=pl.ANY)],
            out_specs=pl.BlockSpec(memory_space=pl.ANY),
            scratch_shapes=[pltpu.SemaphoreType.DMA(())]),
        out_shape=jax.ShapeDtypeStruct(zeros.shape, BF16),
        input_output_aliases={2: 0},
        compiler_params=_params(4 * tm * d * 2 + 4 * 1024 * 1024, 1),
        name="moe_dispatch",
    )(pos_flat, h3, zeros)
    return xs.reshape(n_slots, d)


def _combine_kernel(pos_ref, x_ref, gate_ref, y_hbm, g_ref, *refs, tc, with_norm):
    if with_norm:
        xo_ref, ho_ref, buf, sem = refs
    else:
        xo_ref, buf, sem = refs
    i = pl.program_id(0)

    def issue(tile, slot):
        def body(r, c):
            t = tile * tc + r
            for k in range(2):
                pltpu.make_async_copy(
                    y_hbm.at[pl.ds(pos_ref[2 * t + k], 1), :],
                    buf.at[slot, pl.ds(k * tc + r, 1), :], sem.at[slot]).start(priority=k)
            return c
        lax.fori_loop(0, tc, body, 0, unroll=8)

    @pl.when(i == 0)
    def _prime():
        issue(0, 0)

    @pl.when(i + 1 < pl.num_programs(0))
    def _prefetch():
        issue(i + 1, (i + 1) % 2)

    slot = i % 2
    pltpu.make_async_copy(y_hbm.at[pl.ds(0, 2 * tc), :], buf.at[slot], sem.at[slot]).wait()
    gates = gate_ref[...]
    y = gates[:, 0:1] * buf[slot, 0:tc, :] + gates[:, 1:2] * buf[slot, tc:2 * tc, :]
    xn = x_ref[...] + y
    xo_ref[...] = xn
    if with_norm:
        ho_ref[...] = _rms(xn, g_ref[...]).astype(BF16)


def moe_combine(x, gates, y, pos_flat, g_next, tc=256):
    t, d = x.shape
    with_norm = g_next is not None
    g_arr = (g_next if with_norm else jnp.ones((d,), F32)).reshape(1, d)
    out_specs = [pl.BlockSpec((tc, d), lambda i, pos: (i, 0))]
    out_shape = [jax.ShapeDtypeStruct((t, d), F32)]
    if with_norm:
        out_specs.append(pl.BlockSpec((tc, d), lambda i, pos: (i, 0)))
        out_shape.append(jax.ShapeDtypeStruct((t, d), BF16))
    outs = pl.pallas_call(
        functools.partial(_combine_kernel, tc=tc, with_norm=with_norm),
        grid_spec=pltpu.PrefetchScalarGridSpec(
            num_scalar_prefetch=1,
            grid=(t // tc,),
            in_specs=[pl.BlockSpec((tc, d), lambda i, pos: (i, 0)),
                      pl.BlockSpec((tc, LANES), lambda i, pos: (i, 0)),
                      pl.BlockSpec(memory_space=pl.ANY),
                      pl.BlockSpec((1, d), lambda i, pos: (0, 0))],
            out_specs=out_specs,
            scratch_shapes=[pltpu.VMEM((2, 2 * tc, d), F32), pltpu.SemaphoreType.DMA((2,))]),
        out_shape=out_shape,
        compiler_params=_params(20 * tc * d * 4 + 4 * 1024 * 1024, 1),
        name="moe_combine",
    )(pos_flat, x, gates, y, g_arr)
    return outs if with_norm else (outs[0], None)


def moe_layer(x, g_ffn, w_router, w13, w2, layer, g_next):
    t, d = x.shape
    d_ff = w2.shape[1]
    tm = MOE_TM
    n_tiles = (2 * t) // tm + N_EXPERTS
    n_slots = n_tiles * tm

    h, meta, gates, counts = moe_route(x, g_ffn, w_router)
    cnt = counts[0, :N_EXPERTS]
    tiles_per = (cnt + tm - 1) // tm
    tile_end = jnp.cumsum(tiles_per)
    tile_start = tile_end - tiles_per
    row_start = tile_start * tm
    e1, e2, r1, r2 = meta[:, 0], meta[:, 1], meta[:, 2], meta[:, 3]
    pos = jnp.stack([row_start[e1] + r1, row_start[e2] + r2], axis=1).reshape(-1).astype(I32)
    tile_id = jnp.arange(n_tiles, dtype=I32)
    used = tile_end[-1]
    te = jnp.sum(jnp.minimum(tile_id, used - 1)[:, None] >= tile_end[None, :], axis=1).astype(I32)
    tr = jnp.clip(cnt[te] - (tile_id - tile_start[te]) * tm, 0, tm)
    tr = jnp.where(tile_id < used, tr, 0).astype(I32)
    te = te + layer * N_EXPERTS

    xs = moe_dispatch(h, pos, n_slots)

    tn13 = 1024
    (act,) = grouped_matmul(
        xs, [(w13, 0), (w13, d_ff // tn13)], te, tr, tm=tm, sub=MOE_SUB, tn=tn13,
        n_col_tiles=d_ff // tn13, extras=[], out_dtypes=[BF16], out_cols=d_ff,
        epilogue=_epi_swiglu, name="moe_w13", ragged=True)
    tn2 = 512
    half = tm // 2
    te2 = jnp.repeat(te, 2)
    tr2 = jnp.stack([jnp.minimum(tr, half), jnp.maximum(tr - half, 0)], axis=1).reshape(-1)
    (y,) = grouped_matmul(
        act, [(w2, 0)], te2, tr2, tm=half, sub=MOE_SUB, tn=tn2, n_col_tiles=d // tn2, extras=[],
        out_dtypes=[F32], out_cols=d, epilogue=_epi_plain, name="moe_w2", ragged=True)
    return moe_combine(x, gates, y, pos, g_next)


def conv_layer(x, h, w_in, w_conv, w_out, layer, seq):
    t, d = x.shape
    tn = 512
    tm = 1024
    te, tr = _dense_tiles(t, tm, layer)
    nt = d // tn
    gb, u = grouped_matmul(
        h, [(w_in, 0), (w_in, nt), (w_in, 2 * nt)], te, tr, tm=tm, sub=tm // 2, tn=tn,
        n_col_tiles=nt, extras=[], out_dtypes=[BF16, BF16], out_cols=d,
        epilogue=_epi_conv_in, name="conv_in")
    z = conv_gate(gb, u, w_conv, seq)
    tn = 1024
    (x,) = grouped_matmul(
        z, [(w_out, 0)], te, tr, tm=tm, sub=tm // 2, tn=tn, n_col_tiles=d // tn,
        extras=[(x, (tm, tn), lambda j, i: (i, j))], out_dtypes=[F32], out_cols=d,
        epilogue=_epi_residual, name="conv_out")
    return x


def dense_ffn(x, h, w13, w2, layer):
    t, d = x.shape
    d_ff = w2.shape[1]
    tn = 1024
    tm = 1024
    te, tr = _dense_tiles(t, tm, layer)
    (act,) = grouped_matmul(
        h, [(w13, 0), (w13, d_ff // tn)], te, tr, tm=tm, sub=512, tn=tn,
        n_col_tiles=d_ff // tn, extras=[], out_dtypes=[BF16], out_cols=d_ff,
        epilogue=_epi_swiglu, name="ffn_w13")
    tn = 512
    tm = 512
    te, tr = _dense_tiles(t, tm, layer)
    (x,) = grouped_matmul(
        act, [(w2, 0)], te, tr, tm=tm, sub=tm // 2, tn=tn, n_col_tiles=d // tn,
        extras=[(x, (tm, tn), lambda j, i: (i, j))], out_dtypes=[F32], out_cols=d,
        epilogue=_epi_residual, name="ffn_w2")
    return x


def attention_layer(x, h, cos, sin, w_qkv, q_gain, k_gain, sink, w_out, layer, batch, seq):
    t, d = x.shape
    n_heads = N_KV_HEADS * Q_PER_KV
    tn = 512
    tm = 1024
    heads_per_tile = tn // HEAD_DIM
    qkv_dim = w_qkv.shape[2]
    n_rope_tiles = (n_heads + N_KV_HEADS) // heads_per_tile
    gain_cols = jnp.concatenate([
        jnp.tile(q_gain.astype(F32) * (1.0 / math.sqrt(HEAD_DIM)), n_heads),
        jnp.tile(k_gain.astype(F32), N_KV_HEADS),
        jnp.ones((N_KV_HEADS * HEAD_DIM,), F32)]).reshape(1, qkv_dim)
    te, tr = _dense_tiles(t, tm, layer)
    (qkv,) = grouped_matmul(
        h, [(w_qkv, 0)], te, tr, tm=tm, sub=256, tn=tn, n_col_tiles=qkv_dim // tn,
        extras=[(cos, (tm, HEAD_DIM), lambda j, i: (i, 0)),
                (sin, (tm, HEAD_DIM), lambda j, i: (i, 0)),
                (gain_cols, (1, tn), lambda j, i: (0, j))],
        out_dtypes=[BF16], out_cols=qkv_dim,
        epilogue=functools.partial(_epi_qkv, heads_per_tile=heads_per_tile,
                                   n_rope_tiles=n_rope_tiles),
        name="attn_qkv")
    o = window_attention(qkv, sink, batch, seq)
    tn = 1024
    (x,) = grouped_matmul(
        o, [(w_out, 0)], te, tr, tm=tm, sub=tm // 2, tn=tn, n_col_tiles=d // tn,
        extras=[(x, (tm, tn), lambda j, i: (i, j))], out_dtypes=[F32], out_cols=d,
        epilogue=_epi_residual, name="attn_out")
    return x


def kernel(x, positions, norm_mix, norm_ffn, conv_in, conv_w, conv_out, attn_qkv, q_norm, k_norm,
           attn_sink, attn_out, ffn_w13, ffn_w2, router, moe_w13, moe_w2):
    batch, seq, d = x.shape
    depth = norm_mix.shape[0]
    t = batch * seq
    x = x.reshape(t, d)
    moe_w13 = moe_w13.reshape((-1,) + moe_w13.shape[2:])
    moe_w2 = moe_w2.reshape((-1,) + moe_w2.shape[2:])
    cos, sin = rope_tables(positions.reshape(t))
    h = rms_norm_bf16(x, norm_mix[0])
    for i in range(depth):
        j = i // 2
        if i % 2 == 0:
            x = conv_layer(x, h, conv_in, conv_w[j], conv_out, j, seq)
            h = rms_norm_bf16(x, norm_ffn[i])
            x = dense_ffn(x, h, ffn_w13, ffn_w2, j)
            h = rms_norm_bf16(x, norm_mix[i + 1])
        else:
            x = attention_layer(x, h, cos, sin, attn_qkv, q_norm[j], k_norm[j],
                                attn_sink[j], attn_out, j, batch, seq)
            g_next = norm_mix[i + 1] if i + 1 < depth else None
            x, h = moe_layer(x, norm_ffn[i], router[j], moe_w13, moe_w2, j, g_next)
    return x.reshape(batch, seq, d)
```

```python
import functools
import math

import jax
import jax.numpy as jnp
from jax import lax
from jax.experimental import pallas as pl
from jax.experimental.pallas import tpu as pltpu

F32 = jnp.float32
BF16 = jnp.bfloat16
I32 = jnp.int32

HEAD_DIM = 128
N_KV_HEADS = 4
Q_PER_KV = 4
WINDOW = 128
ATTN_BLOCK = 128
ROPE_THETA = 10000.0
N_EXPERTS = 8
RMS_EPS = 1e-6
MASK_VALUE = -1e30

LANES = 128
BF16_SUBLANES = 16
VMEM_BUDGET = 60000 * 1024

MOE_TM = 1024
MOE_SUB_W13 = 256
MOE_SUB_W2 = 128
MOE_TM_W2 = 512


def _params(vmem_bytes, n_grid):
    return pltpu.CompilerParams(
        dimension_semantics=("arbitrary",) * n_grid,
        vmem_limit_bytes=int(min(vmem_bytes, VMEM_BUDGET)))


def _nbytes(shape, dtype):
    return math.prod(shape) * jnp.dtype(dtype).itemsize


def _rms(x, g):
    ms = jnp.mean(x * x, axis=-1, keepdims=True)
    return x * lax.rsqrt(ms + RMS_EPS) * g


def _norm_kernel(x_ref, g_ref, o_ref):
    o_ref[...] = _rms(x_ref[...], g_ref[...]).astype(o_ref.dtype)


def rms_norm_bf16(x, g, tm=512):
    t, d = x.shape
    return pl.pallas_call(
        _norm_kernel,
        grid=(t // tm,),
        in_specs=[pl.BlockSpec((tm, d), lambda i: (i, 0)),
                  pl.BlockSpec((1, d), lambda i: (0, 0))],
        out_specs=pl.BlockSpec((tm, d), lambda i: (i, 0)),
        out_shape=jax.ShapeDtypeStruct((t, d), BF16),
        compiler_params=_params(6 * tm * d * 4, 1),
        name="rms_norm",
    )(x, g.reshape(1, d))


def _gmm_kernel(te_ref, tr_ref, nx_ref, ti_ref, x_ref, *refs, n_w, n_extra, n_out, col_offs, tn,
                sub, ragged, epilogue):
    del ti_ref
    w_refs = refs[:n_w]
    e_refs = refs[n_w:n_w + n_extra]
    o_refs = refs[n_w + n_extra:n_w + n_extra + n_out]
    stage, wb, sem = refs[n_w + n_extra + n_out:]
    j = pl.program_id(0)
    i = pl.program_id(1)
    n_j = pl.num_programs(0)
    tm = x_ref.shape[0]
    cur = te_ref[i]
    first_of_group = jnp.logical_or(i == 0, cur != te_ref[jnp.maximum(i - 1, 0)])

    def weight_copy(v, group, jj):
        col = pl.multiple_of((jj + col_offs[v]) * tn, tn)
        return pltpu.make_async_copy(
            w_refs[v].at[group, :, pl.ds(col, tn)], stage.at[v], sem.at[v])

    @pl.when(jnp.logical_and(j == 0, i == 0))
    def _prime():
        for v in range(n_w):
            weight_copy(v, cur, j).start()

    @pl.when(first_of_group)
    def _load_weights():
        for v in range(n_w):
            weight_copy(v, cur, j).wait()
            wb[v] = stage[v].astype(BF16)
        nxt = nx_ref[i]

        @pl.when(nxt >= 0)
        def _():
            for v in range(n_w):
                weight_copy(v, nxt, j).start()

        @pl.when(jnp.logical_and(nxt < 0, j + 1 < n_j))
        def _():
            for v in range(n_w):
                weight_copy(v, te_ref[0], j + 1).start()

    def compute(rows):
        xb = x_ref[rows, :]
        accs = [jnp.dot(xb, wb[v], preferred_element_type=F32) for v in range(n_w)]
        epilogue(j, rows, accs, e_refs, o_refs)

    n_sub = tm // sub
    sub_rows = [slice(s * sub, (s + 1) * sub) for s in range(n_sub)]

    def tile_body(n_live):
        for s in range(n_live):
            compute(sub_rows[s])
        for s in range(n_live, n_sub):
            for o in o_refs:
                o[sub_rows[s], :] = jnp.zeros((sub, o.shape[1]), o.dtype)

    if not ragged:
        tile_body(n_sub)
    else:
        live = (tr_ref[i] + (sub - 1)) // sub
        for n_live in range(n_sub + 1):
            pl.when(live == n_live)(functools.partial(tile_body, n_live))


def grouped_matmul(x, weights, te, tr, *, tm, sub, tn, n_col_tiles, extras, out_dtypes,
                   out_cols, epilogue, name, ragged=False):
    rows, k = x.shape
    n_w, n_extra, n_out = len(weights), len(extras), len(out_dtypes)
    later = jnp.where(te[None, :] > te[:, None], te[None, :], jnp.iinfo(jnp.int32).max)
    nx = jnp.min(later, axis=1)
    nx = jnp.where(nx == jnp.iinfo(jnp.int32).max, -1, nx).astype(I32)
    tile_id = jnp.arange(rows // tm, dtype=I32)
    has_rows_before = jnp.logical_and(tile_id[None, :] <= tile_id[:, None], tr[None, :] > 0)
    ti = jnp.max(jnp.where(has_rows_before, tile_id[None, :], 0), axis=1).astype(I32)

    in_specs = [pl.BlockSpec((tm, k), lambda j, i, te, tr, nx, ti: (ti[i], 0))]
    in_specs += [pl.BlockSpec(memory_space=pl.ANY) for _ in weights]
    for _, bshape, imap in extras:
        in_specs.append(pl.BlockSpec(
            bshape, lambda j, i, te, tr, nx, ti, imap=imap: imap(j, ti[i])))
    out_specs = [pl.BlockSpec((tm, tn), lambda j, i, *_: (i, j)) for _ in out_dtypes]
    out_shape = [jax.ShapeDtypeStruct((rows, out_cols), dt) for dt in out_dtypes]

    vmem = 2 * _nbytes((tm, k), x.dtype)
    vmem += n_w * (_nbytes((k, tn), F32) + _nbytes((k, tn), BF16))
    vmem += sum(2 * _nbytes(b, a.dtype) for a, b, _ in extras)
    vmem += sum(2 * _nbytes((tm, tn), dt) for dt in out_dtypes)
    vmem += (n_w + 3) * _nbytes((tm, tn), F32)
    vmem += _nbytes((tm, k), x.dtype) + 4 * 1024 * 1024

    return pl.pallas_call(
        functools.partial(_gmm_kernel, n_w=n_w, n_extra=n_extra, n_out=n_out,
                          col_offs=tuple(off for _, off in weights), tn=tn, sub=sub,
                          ragged=ragged, epilogue=epilogue),
        grid_spec=pltpu.PrefetchScalarGridSpec(
            num_scalar_prefetch=4,
            grid=(n_col_tiles, rows // tm),
            in_specs=in_specs,
            out_specs=out_specs,
            scratch_shapes=[pltpu.VMEM((n_w, k, tn), F32),
                            pltpu.VMEM((n_w, k, tn), BF16),
                            pltpu.SemaphoreType.DMA((n_w,))]),
        out_shape=out_shape,
        compiler_params=_params(vmem, 2),
        name=name,
    )(te, tr, nx, ti, x, *[w for w, _ in weights], *[a for a, _, _ in extras])


def _dense_tiles(rows, tm, group):
    n = rows // tm
    return jnp.full((n,), group, I32), jnp.full((n,), tm, I32)


def _epi_swiglu(j, rows, accs, e_refs, o_refs):
    g, u = accs
    o_refs[0][rows, :] = (g * jax.nn.sigmoid(g) * u).astype(BF16)


def _epi_residual(j, rows, accs, e_refs, o_refs):
    o_refs[0][rows, :] = e_refs[0][rows, :] + accs[0]


def _epi_plain(j, rows, accs, e_refs, o_refs):
    o_refs[0][rows, :] = accs[0]


def _epi_conv_in(j, rows, accs, e_refs, o_refs):
    gate_b, gate_c, val = accs
    o_refs[0][rows, :] = gate_b.astype(BF16)
    o_refs[1][rows, :] = (gate_c * val).astype(BF16)


def _epi_qkv(j, rows, accs, e_refs, o_refs, *, heads_per_tile, n_rope_tiles):
    acc = accs[0]
    cos_ref, sin_ref, gain_ref = e_refs
    o = o_refs[0]
    cos = cos_ref[rows, :]
    sin = sin_ref[rows, :]
    is_qk = j < n_rope_tiles
    for hh in range(heads_per_tile):
        sl = slice(hh * HEAD_DIM, (hh + 1) * HEAD_DIM)
        a = acc[:, sl]
        y = _rms(a, gain_ref[:, sl])
        rot = pltpu.roll(y, HEAD_DIM // 2, axis=1)
        o[rows, sl] = jnp.where(is_qk, y * cos + rot * sin, a).astype(BF16)


def _conv_kernel(gb_ref, u_ref, up_ref, un_ref, w_ref, o_ref, *, tm, seq):
    i = pl.program_id(0)
    u = u_ref[...].astype(F32)
    w = w_ref[...]
    row = lax.broadcasted_iota(I32, u.shape, 0)
    starts_seq = (i * tm) % seq == 0
    ends_seq = ((i + 1) * tm) % seq == 0
    prev_row = up_ref[...].astype(F32)[BF16_SUBLANES - 1:BF16_SUBLANES, :]
    next_row = un_ref[...].astype(F32)[0:1, :]
    prev_row = jnp.where(starts_seq, 0.0, prev_row)
    next_row = jnp.where(ends_seq, 0.0, next_row)
    u_prev = jnp.where(row == 0, prev_row, pltpu.roll(u, 1, axis=0))
    u_next = jnp.where(row == tm - 1, next_row, pltpu.roll(u, tm - 1, axis=0))
    conv = w[0:1, :] * u_prev + w[1:2, :] * u + w[2:3, :] * u_next
    o_ref[...] = (gb_ref[...].astype(F32) * conv).astype(BF16)


def conv_gate(gb, u, w_conv, seq, tm=512):
    t, d = u.shape
    hb = tm // BF16_SUBLANES
    n_halo = t // BF16_SUBLANES
    return pl.pallas_call(
        functools.partial(_conv_kernel, tm=tm, seq=seq),
        grid=(t // tm,),
        in_specs=[pl.BlockSpec((tm, d), lambda i: (i, 0)),
                  pl.BlockSpec((tm, d), lambda i: (i, 0)),
                  pl.BlockSpec((BF16_SUBLANES, d), lambda i: (jnp.maximum(i * hb - 1, 0), 0)),
                  pl.BlockSpec((BF16_SUBLANES, d),
                               lambda i: (jnp.minimum((i + 1) * hb, n_halo - 1), 0)),
                  pl.BlockSpec((3, d), lambda i: (0, 0))],
        out_specs=pl.BlockSpec((tm, d), lambda i: (i, 0)),
        out_shape=jax.ShapeDtypeStruct((t, d), BF16),
        compiler_params=_params(12 * tm * d * 4, 1),
        name="conv_gate",
    )(gb, u, u, u, w_conv)


def _rope_kernel(pos_ref, invf_ref, sign_ref, cos_ref, sin_ref):
    ang = pos_ref[...].astype(F32) * invf_ref[...]
    cos_ref[...] = jnp.cos(ang)
    sin_ref[...] = jnp.sin(ang) * sign_ref[...]


def rope_tables(positions, tm=512):
    t = positions.shape[0]
    half = HEAD_DIM // 2
    inv_freq = ROPE_THETA ** (-jnp.arange(0, half, dtype=F32) / half)
    invf = jnp.concatenate([inv_freq, inv_freq]).reshape(1, HEAD_DIM)
    sign = jnp.concatenate([-jnp.ones((half,), F32), jnp.ones((half,), F32)]).reshape(1, HEAD_DIM)
    return pl.pallas_call(
        _rope_kernel,
        grid=(t // tm,),
        in_specs=[pl.BlockSpec((tm, 1), lambda i: (i, 0)),
                  pl.BlockSpec((1, HEAD_DIM), lambda i: (0, 0)),
                  pl.BlockSpec((1, HEAD_DIM), lambda i: (0, 0))],
        out_specs=[pl.BlockSpec((tm, HEAD_DIM), lambda i: (i, 0))] * 2,
        out_shape=[jax.ShapeDtypeStruct((t, HEAD_DIM), F32)] * 2,
        compiler_params=_params(16 * tm * HEAD_DIM * 4, 1),
        name="rope_tables",
    )(positions.reshape(t, 1), invf, sign)


ATTN_ROWS = Q_PER_KV * ATTN_BLOCK
ATTN_KEYS = 3 * ATTN_BLOCK
ATTN_UNROLL = 4


def _attn_kernel(q_ref, k_ref, v_ref, sink_ref, o_ref, bias_ref):
    seq = q_ref.shape[0]
    sink = jnp.broadcast_to(sink_ref[...], (ATTN_ROWS, HEAD_DIM))
    q_off = lax.broadcasted_iota(I32, (ATTN_BLOCK, ATTN_KEYS), 0)
    k_off = lax.broadcasted_iota(I32, (ATTN_BLOCK, ATTN_KEYS), 1)
    for c in range(3):
        inside = jnp.abs(k_off - q_off - c * ATTN_BLOCK) <= WINDOW
        bias_ref[c] = jnp.where(inside, 0.0, MASK_VALUE)
    ones = jnp.ones((ATTN_KEYS, HEAD_DIM), BF16)

    def block(n):
        q0 = pl.multiple_of(n * ATTN_BLOCK, ATTN_BLOCK)
        k0 = pl.multiple_of(jnp.clip((n - 1) * ATTN_BLOCK, 0, seq - ATTN_KEYS), ATTN_BLOCK)
        qb = q_ref[pl.ds(q0, ATTN_BLOCK), :]
        qs = jnp.concatenate(
            [qb[:, g * HEAD_DIM:(g + 1) * HEAD_DIM] for g in range(Q_PER_KV)], axis=0)
        kw = k_ref[pl.ds(k0, ATTN_KEYS), :]
        v_ext = jnp.concatenate([v_ref[pl.ds(k0, ATTN_KEYS), :], ones], axis=1)
        s_all = lax.dot_general(qs, kw, (((1,), (1,)), ((), ())), preferred_element_type=F32)
        bias = bias_ref[(q0 - k0) // ATTN_BLOCK]
        ms, ps = [], []
        for g in range(Q_PER_KV):
            rows = slice(g * ATTN_BLOCK, (g + 1) * ATTN_BLOCK)
            s = s_all[rows, :] + bias
            m = jnp.maximum(jnp.max(s, axis=-1, keepdims=True), sink[rows, :])
            p = jnp.exp(s - jnp.concatenate([m] * (ATTN_KEYS // HEAD_DIM), axis=1))
            ms.append(m)
            ps.append(p.astype(BF16))
        pv = jnp.dot(jnp.concatenate(ps, axis=0), v_ext, preferred_element_type=F32)
        for g in range(Q_PER_KV):
            rows = slice(g * ATTN_BLOCK, (g + 1) * ATTN_BLOCK)
            denom = pv[rows, HEAD_DIM:] + jnp.exp(sink[rows, :] - ms[g])
            o = pv[rows, :HEAD_DIM] / denom
            o_ref[pl.ds(q0, ATTN_BLOCK), g * HEAD_DIM:(g + 1) * HEAD_DIM] = o.astype(BF16)

    def body(n2, carry):
        for r in range(ATTN_UNROLL):
            block(n2 * ATTN_UNROLL + r)
        return carry

    lax.fori_loop(0, seq // (ATTN_BLOCK * ATTN_UNROLL), body, 0)


def window_attention(qkv, sink, batch, seq):
    t = qkv.shape[0]
    n_heads = N_KV_HEADS * Q_PER_KV
    gw = Q_PER_KV * HEAD_DIM
    sink_col = jnp.repeat(sink.astype(F32).reshape(N_KV_HEADS, Q_PER_KV), ATTN_BLOCK, axis=1)
    sink_col = sink_col.reshape(N_KV_HEADS, ATTN_ROWS, 1)
    return pl.pallas_call(
        _attn_kernel,
        grid=(batch, N_KV_HEADS),
        in_specs=[pl.BlockSpec((seq, gw), lambda b, h: (b, h)),
                  pl.BlockSpec((seq, HEAD_DIM), lambda b, h: (b, n_heads + h)),
                  pl.BlockSpec((seq, HEAD_DIM), lambda b, h: (b, n_heads + N_KV_HEADS + h)),
                  pl.BlockSpec((None, ATTN_ROWS, 1), lambda b, h: (h, 0, 0))],
        out_specs=pl.BlockSpec((seq, gw), lambda b, h: (b, h)),
        out_shape=jax.ShapeDtypeStruct((t, n_heads * HEAD_DIM), BF16),
        scratch_shapes=[pltpu.VMEM((3, ATTN_BLOCK, ATTN_KEYS), F32)],
        compiler_params=_params(6 * seq * gw * 2 + 24 * 1024 * 1024, 2),
        name="window_attention",
    )(qkv, qkv, qkv, sink_col)


def _route_kernel(x_ref, g_ref, r_ref, h_ref, meta_ref, gate_ref, cnt_ref, carry_ref, *, tm):
    i = pl.program_id(0)

    @pl.when(i == 0)
    def _init():
        carry_ref[...] = jnp.zeros(carry_ref.shape, carry_ref.dtype)

    h = _rms(x_ref[...], g_ref[...])
    h_ref[...] = h.astype(BF16)
    r = r_ref[...]
    h_hi = h.astype(BF16)
    h_lo = (h - h_hi.astype(F32)).astype(BF16)
    r_hi = r.astype(BF16)
    r_lo = (r - r_hi.astype(F32)).astype(BF16)
    logits = (jnp.dot(h_hi, r_hi, preferred_element_type=F32)
              + (jnp.dot(h_hi, r_lo, preferred_element_type=F32)
                 + jnp.dot(h_lo, r_hi, preferred_element_type=F32)))
    lane = lax.broadcasted_iota(I32, logits.shape, 1)
    lane_f = lane.astype(F32)
    neg_inf = jnp.float32(-jnp.inf)
    logits = jnp.where(lane < N_EXPERTS, logits, neg_inf)
    v1 = jnp.max(logits, axis=-1, keepdims=True)
    i1 = jnp.min(jnp.where(logits == v1, lane_f, float(LANES)), axis=-1, keepdims=True)
    rest = jnp.where(lane_f == i1, neg_inf, logits)
    v2 = jnp.max(rest, axis=-1, keepdims=True)
    i2 = jnp.min(jnp.where(rest == v2, lane_f, float(LANES)), axis=-1, keepdims=True)
    e21 = jnp.exp(v2 - v1)
    g1 = 1.0 / (1.0 + e21)
    g2 = e21 / (1.0 + e21)

    sel1 = lane_f == i1
    sel2 = lane_f == i2
    onehot = jnp.logical_or(sel1, sel2)
    r_io = lax.broadcasted_iota(I32, (tm, tm), 0)
    c_io = lax.broadcasted_iota(I32, (tm, tm), 1)
    strict_lower = (c_io < r_io).astype(BF16)
    carry = carry_ref[...]
    before = jnp.dot(strict_lower, onehot.astype(BF16), preferred_element_type=F32) + carry
    rank1 = jnp.sum(jnp.where(sel1, before, 0.0), axis=-1, keepdims=True)
    rank2 = jnp.sum(jnp.where(sel2, before, 0.0), axis=-1, keepdims=True)
    carry = carry + jnp.sum(onehot.astype(F32), axis=0, keepdims=True)
    carry_ref[...] = carry
    cnt_ref[...] = carry.astype(I32)

    meta = jnp.where(lane == 0, i1, jnp.where(lane == 1, i2,
           jnp.where(lane == 2, rank1, jnp.where(lane == 3, rank2, 0.0))))
    meta_ref[...] = meta.astype(I32)
    gate_ref[...] = jnp.where(lane == 0, g1, jnp.where(lane == 1, g2, 0.0))


def moe_route(x, g, w_router, tm=512):
    t, d = x.shape
    r_pad = jnp.zeros((d, LANES), F32).at[:, :N_EXPERTS].set(w_router)
    return pl.pallas_call(
        functools.partial(_route_kernel, tm=tm),
        grid=(t // tm,),
        in_specs=[pl.BlockSpec((tm, d), lambda i: (i, 0)),
                  pl.BlockSpec((1, d), lambda i: (0, 0)),
                  pl.BlockSpec((d, LANES), lambda i: (0, 0))],
        out_specs=[pl.BlockSpec((tm, d), lambda i: (i, 0)),
                   pl.BlockSpec((tm, LANES), lambda i: (i, 0)),
                   pl.BlockSpec((tm, LANES), lambda i: (i, 0)),
                   pl.BlockSpec((1, LANES), lambda i: (0, 0))],
        out_shape=[jax.ShapeDtypeStruct((t, d), BF16),
                   jax.ShapeDtypeStruct((t, LANES), I32),
                   jax.ShapeDtypeStruct((t, LANES), F32),
                   jax.ShapeDtypeStruct((1, LANES), I32)],
        scratch_shapes=[pltpu.VMEM((1, LANES), F32)],
        compiler_params=_params(10 * tm * d * 4 + 8 * 1024 * 1024, 1),
        name="moe_route",
    )(x, g.reshape(1, d), r_pad)


DISPATCH_TM = 512
DISPATCH_UNROLL = 8


def _dispatch_kernel(pos_ref, h_ref, zeros_hbm, xs_hbm, sem):
    del zeros_hbm
    tm = h_ref.shape[0]
    base = pl.program_id(0) * tm

    def body(r, c):
        for k in range(2):
            pltpu.make_async_copy(
                h_ref.at[r], xs_hbm.at[pos_ref[2 * (base + r) + k]], sem).start(priority=k)
        return c
    lax.fori_loop(0, tm, body, 0, unroll=DISPATCH_UNROLL)
    for k in range(2):
        pltpu.make_async_copy(h_ref, xs_hbm.at[pl.ds(0, tm)], sem).wait()


def moe_dispatch(h, pos_flat, n_slots):
    t, d = h.shape
    h3 = h.reshape(t, d // LANES, LANES)
    zeros = jnp.zeros((n_slots, d // LANES, LANES), BF16)
    tm = DISPATCH_TM
    xs = pl.pallas_call(
        _dispatch_kernel,
        grid_spec=pltpu.PrefetchScalarGridSpec(
            num_scalar_prefetch=1,
            grid=(t // tm,),
            in_specs=[pl.BlockSpec((tm, d // LANES, LANES), lambda i, pos: (i, 0, 0)),
                      pl.BlockSpec(memory_space=pl.ANY)],
            out_specs=pl.BlockSpec(memory_space=pl.ANY),
            scratch_shapes=[pltpu.SemaphoreType.DMA(())]),
        out_shape=jax.ShapeDtypeStruct(zeros.shape, BF16),
        input_output_aliases={2: 0},
        compiler_params=_params(4 * tm * d * 2 + 4 * 1024 * 1024, 1),
        name="moe_dispatch",
    )(pos_flat, h3, zeros)
    return xs.reshape(n_slots, d)


def _combine_kernel(pos_ref, x_ref, gate_ref, y_hbm, g_ref, *refs, tc, with_norm):
    if with_norm:
        xo_ref, ho_ref, buf, sem = refs
    else:
        xo_ref, buf, sem = refs
    i = pl.program_id(0)

    def issue(tile, slot):
        def body(r, c):
            t = tile * tc + r
            for k in range(2):
                pltpu.make_async_copy(
                    y_hbm.at[pl.ds(pos_ref[2 * t + k], 1), :],
                    buf.at[slot, pl.ds(k * tc + r, 1), :], sem.at[slot]).start(priority=k)
            return c
        lax.fori_loop(0, tc, body, 0, unroll=8)

    @pl.when(i == 0)
    def _prime():
        issue(0, 0)

    @pl.when(i + 1 < pl.num_programs(0))
    def _prefetch():
        issue(i + 1, (i + 1) % 2)

    slot = i % 2
    pltpu.make_async_copy(y_hbm.at[pl.ds(0, 2 * tc), :], buf.at[slot], sem.at[slot]).wait()
    gates = gate_ref[...]
    y = gates[:, 0:1] * buf[slot, 0:tc, :] + gates[:, 1:2] * buf[slot, tc:2 * tc, :]
    xn = x_ref[...] + y
    xo_ref[...] = xn
    if with_norm:
        ho_ref[...] = _rms(xn, g_ref[...]).astype(BF16)


def moe_combine(x, gates, y, pos_flat, g_next, tc=256):
    t, d = x.shape
    with_norm = g_next is not None
    g_arr = (g_next if with_norm else jnp.ones((d,), F32)).reshape(1, d)
    out_specs = [pl.BlockSpec((tc, d), lambda i, pos: (i, 0))]
    out_shape = [jax.ShapeDtypeStruct((t, d), F32)]
    if with_norm:
        out_specs.append(pl.BlockSpec((tc, d), lambda i, pos: (i, 0)))
        out_shape.append(jax.ShapeDtypeStruct((t, d), BF16))
    outs = pl.pallas_call(
        functools.partial(_combine_kernel, tc=tc, with_norm=with_norm),
        grid_spec=pltpu.PrefetchScalarGridSpec(
            num_scalar_prefetch=1,
            grid=(t // tc,),
            in_specs=[pl.BlockSpec((tc, d), lambda i, pos: (i, 0)),
                      pl.BlockSpec((tc, LANES), lambda i, pos: (i, 0)),
                      pl.BlockSpec(memory_space=pl.ANY),
                      pl.BlockSpec((1, d), lambda i, pos: (0, 0))],
            out_specs=out_specs,
            scratch_shapes=[pltpu.VMEM((2, 2 * tc, d), F32), pltpu.SemaphoreType.DMA((2,))]),
        out_shape=out_shape,
        compiler_params=_params(20 * tc * d * 4 + 4 * 1024 * 1024, 1),
        name="moe_combine",
    )(pos_flat, x, gates, y, g_arr)
    return outs if with_norm else (outs[0], None)


def moe_layer(x, g_ffn, w_router, w13, w2, layer, g_next):
    t, d = x.shape
    d_ff = w2.shape[1]
    tm = MOE_TM
    n_tiles = (2 * t) // tm + N_EXPERTS
    n_slots = n_tiles * tm

    h, meta, gates, counts = moe_route(x, g_ffn, w_router)
    cnt = counts[0, :N_EXPERTS]
    tiles_per = (cnt + tm - 1) // tm
    tile_end = jnp.cumsum(tiles_per)
    tile_start = tile_end - tiles_per
    row_start = tile_start * tm
    e1, e2, r1, r2 = meta[:, 0], meta[:, 1], meta[:, 2], meta[:, 3]
    pos = jnp.stack([row_start[e1] + r1, row_start[e2] + r2], axis=1).reshape(-1).astype(I32)
    tile_id = jnp.arange(n_tiles, dtype=I32)
    used = tile_end[-1]
    te = jnp.sum(jnp.minimum(tile_id, used - 1)[:, None] >= tile_end[None, :], axis=1).astype(I32)
    tr = jnp.clip(cnt[te] - (tile_id - tile_start[te]) * tm, 0, tm)
    tr = jnp.where(tile_id < used, tr, 0).astype(I32)
    te = te + layer * N_EXPERTS

    xs = moe_dispatch(h, pos, n_slots)

    tn13 = 1024
    (act,) = grouped_matmul(
        xs, [(w13, 0), (w13, d_ff // tn13)], te, tr, tm=tm, sub=MOE_SUB_W13, tn=tn13,
        n_col_tiles=d_ff // tn13, extras=[], out_dtypes=[BF16], out_cols=d_ff,
        epilogue=_epi_swiglu, name="moe_w13", ragged=True)
    tn2 = 512
    tm2 = MOE_TM_W2
    parts = tm // tm2
    te2 = jnp.repeat(te, parts)
    tr2 = jnp.clip(tr[:, None] - jnp.arange(parts, dtype=I32)[None, :] * tm2, 0, tm2).reshape(-1)
    (y,) = grouped_matmul(
        act, [(w2, 0)], te2, tr2, tm=tm2, sub=MOE_SUB_W2, tn=tn2, n_col_tiles=d // tn2, extras=[],
        out_dtypes=[F32], out_cols=d, epilogue=_epi_plain, name="moe_w2", ragged=True)
    return moe_combine(x, gates, y, pos, g_next)


def conv_layer(x, h, w_in, w_conv, w_out, layer, seq):
    t, d = x.shape
    tn = 512
    tm = 1024
    te, tr = _dense_tiles(t, tm, layer)
    nt = d // tn
    gb, u = grouped_matmul(
        h, [(w_in, 0), (w_in, nt), (w_in, 2 * nt)], te, tr, tm=tm, sub=tm // 2, tn=tn,
        n_col_tiles=nt, extras=[], out_dtypes=[BF16, BF16], out_cols=d,
        epilogue=_epi_conv_in, name="conv_in")
    z = conv_gate(gb, u, w_conv, seq)
    tn = 1024
    (x,) = grouped_matmul(
        z, [(w_out, 0)], te, tr, tm=tm, sub=tm // 2, tn=tn, n_col_tiles=d // tn,
        extras=[(x, (tm, tn), lambda j, i: (i, j))], out_dtypes=[F32], out_cols=d,
        epilogue=_epi_residual, name="conv_out")
    return x


def dense_ffn(x, h, w13, w2, layer):
    t, d = x.shape
    d_ff = w2.shape[1]
    tn = 1024
    tm = 1024
    te, tr = _dense_tiles(t, tm, layer)
    (act,) = grouped_matmul(
        h, [(w13, 0), (w13, d_ff // tn)], te, tr, tm=tm, sub=512, tn=tn,
        n_col_tiles=d_ff // tn, extras=[], out_dtypes=[BF16], out_cols=d_ff,
        epilogue=_epi_swiglu, name="ffn_w13")
    tn = 512
    tm = 512
    te, tr = _dense_tiles(t, tm, layer)
    (x,) = grouped_matmul(
        act, [(w2, 0)], te, tr, tm=tm, sub=tm // 2, tn=tn, n_col_tiles=d // tn,
        extras=[(x, (tm, tn), lambda j, i: (i, j))], out_dtypes=[F32], out_cols=d,
        epilogue=_epi_residual, name="ffn_w2")
    return x


def attention_layer(x, h, cos, sin, w_qkv, q_gain, k_gain, sink, w_out, layer, batch, seq):
    t, d = x.shape
    n_heads = N_KV_HEADS * Q_PER_KV
    tn = 512
    tm = 1024
    heads_per_tile = tn // HEAD_DIM
    qkv_dim = w_qkv.shape[2]
    n_rope_tiles = (n_heads + N_KV_HEADS) // heads_per_tile
    gain_cols = jnp.concatenate([
        jnp.tile(q_gain.astype(F32) * (1.0 / math.sqrt(HEAD_DIM)), n_heads),
        jnp.tile(k_gain.astype(F32), N_KV_HEADS),
        jnp.ones((N_KV_HEADS * HEAD_DIM,), F32)]).reshape(1, qkv_dim)
    te, tr = _dense_tiles(t, tm, layer)
    (qkv,) = grouped_matmul(
        h, [(w_qkv, 0)], te, tr, tm=tm, sub=256, tn=tn, n_col_tiles=qkv_dim // tn,
        extras=[(cos, (tm, HEAD_DIM), lambda j, i: (i, 0)),
                (sin, (tm, HEAD_DIM), lambda j, i: (i, 0)),
                (gain_cols, (1, tn), lambda j, i: (0, j))],
        out_dtypes=[BF16], out_cols=qkv_dim,
        epilogue=functools.partial(_epi_qkv, heads_per_tile=heads_per_tile,
                                   n_rope_tiles=n_rope_tiles),
        name="attn_qkv")
    o = window_attention(qkv, sink, batch, seq)
    tn = 1024
    (x,) = grouped_matmul(
        o, [(w_out, 0)], te, tr, tm=tm, sub=tm // 2, tn=tn, n_col_tiles=d // tn,
        extras=[(x, (tm, tn), lambda j, i: (i, j))], out_dtypes=[F32], out_cols=d,
        epilogue=_epi_residual, name="attn_out")
    return x


def kernel(x, positions, norm_mix, norm_ffn, conv_in, conv_w, conv_out, attn_qkv, q_norm, k_norm,
           attn_sink, attn_out, ffn_w13, ffn_w2, router, moe_w13, moe_w2):
    batch, seq, d = x.shape
    depth = norm_mix.shape[0]
    t = batch * seq
    x = x.reshape(t, d)
    moe_w13 = moe_w13.reshape((-1,) + moe_w13.shape[2:])
    moe_w2 = moe_w2.reshape((-1,) + moe_w2.shape[2:])
    cos, sin = rope_tables(positions.reshape(t))
    h = rms_norm_bf16(x, norm_mix[0])
    for i in range(depth):
        j = i // 2
        if i % 2 == 0:
            x = conv_layer(x, h, conv_in, conv_w[j], conv_out, j, seq)
            h = rms_norm_bf16(x, norm_ffn[i])
            x = dense_ffn(x, h, ffn_w13, ffn_w2, j)
            h = rms_norm_bf16(x, norm_mix[i + 1])
        else:
            x = attention_layer(x, h, cos, sin, attn_qkv, q_norm[j], k_norm[j],
                                attn_sink[j], attn_out, j, batch, seq)
            g_next = norm_mix[i + 1] if i + 1 < depth else None
            x, h = moe_layer(x, norm_ffn[i], router[j], moe_w13, moe_w2, j, g_next)
    return x.reshape(batch, seq, d)
```

```python
import functools
import math

import jax
import jax.numpy as jnp
from jax import lax
from jax.experimental import pallas as pl
from jax.experimental.pallas import tpu as pltpu

F32 = jnp.float32
BF16 = jnp.bfloat16
I32 = jnp.int32

HEAD_DIM = 128
N_KV_HEADS = 4
Q_PER_KV = 4
WINDOW = 128
ATTN_BLOCK = 128
ROPE_THETA = 10000.0
N_EXPERTS = 8
RMS_EPS = 1e-6
MASK_VALUE = -1e30

LANES = 128
BF16_SUBLANES = 16
VMEM_BUDGET = 60000 * 1024

MOE_TM = 1024
MOE_SUB_W13 = 256
MOE_SUB_W2 = 128
MOE_TM_W2 = 512


def _params(vmem_bytes, n_grid):
    return pltpu.CompilerParams(
        dimension_semantics=("arbitrary",) * n_grid,
        vmem_limit_bytes=int(min(vmem_bytes, VMEM_BUDGET)))


def _nbytes(shape, dtype):
    return math.prod(shape) * jnp.dtype(dtype).itemsize


def _rms(x, g):
    ms = jnp.mean(x * x, axis=-1, keepdims=True)
    return x * lax.rsqrt(ms + RMS_EPS) * g


def _norm_kernel(x_ref, g_ref, o_ref):
    o_ref[...] = _rms(x_ref[...], g_ref[...]).astype(o_ref.dtype)


def rms_norm_bf16(x, g, tm=512):
    t, d = x.shape
    return pl.pallas_call(
        _norm_kernel,
        grid=(t // tm,),
        in_specs=[pl.BlockSpec((tm, d), lambda i: (i, 0)),
                  pl.BlockSpec((1, d), lambda i: (0, 0))],
        out_specs=pl.BlockSpec((tm, d), lambda i: (i, 0)),
        out_shape=jax.ShapeDtypeStruct((t, d), BF16),
        compiler_params=_params(6 * tm * d * 4, 1),
        name="rms_norm",
    )(x, g.reshape(1, d))


def _gmm_kernel(te_ref, tr_ref, nx_ref, ti_ref, go_ref, x_ref, *refs, n_w, n_extra, n_out,
                col_offs, tn, sub, ragged, epilogue):
    del ti_ref
    w_refs = refs[:n_w]
    e_refs = refs[n_w:n_w + n_extra]
    o_refs = refs[n_w + n_extra:n_w + n_extra + n_out]
    stage, sem = refs[n_w + n_extra + n_out:]
    j = pl.program_id(0)
    i = pl.program_id(1)
    n_j = pl.num_programs(0)
    n_i = pl.num_programs(1)
    tm = x_ref.shape[0]
    cur = te_ref[i]
    first_of_group = jnp.logical_or(i == 0, cur != te_ref[jnp.maximum(i - 1, 0)])
    slot = (j * go_ref[n_i] + go_ref[i]) % 2

    def weight_copy(v, group, jj, to_slot):
        col = pl.multiple_of((jj + col_offs[v]) * tn, tn)
        return pltpu.make_async_copy(
            w_refs[v].at[group, :, pl.ds(col, tn)], stage.at[to_slot, v], sem.at[to_slot, v])

    @pl.when(jnp.logical_and(j == 0, i == 0))
    def _prime():
        for v in range(n_w):
            weight_copy(v, cur, j, slot).start()

    @pl.when(first_of_group)
    def _switch_weights():
        for v in range(n_w):
            weight_copy(v, cur, j, slot).wait()
        nxt = nx_ref[i]

        @pl.when(nxt >= 0)
        def _():
            for v in range(n_w):
                weight_copy(v, nxt, j, 1 - slot).start()

        @pl.when(jnp.logical_and(nxt < 0, j + 1 < n_j))
        def _():
            for v in range(n_w):
                weight_copy(v, te_ref[0], j + 1, 1 - slot).start()

    def compute(rows):
        xb = x_ref[rows, :]
        accs = [lax.dot_general(xb, stage[slot, v], (((1,), (0,)), ((), ())),
                                preferred_element_type=F32) for v in range(n_w)]
        epilogue(j, rows, accs, e_refs, o_refs)

    n_sub = tm // sub
    sub_rows = [slice(s * sub, (s + 1) * sub) for s in range(n_sub)]

    def tile_body(n_live):
        for s in range(n_live):
            compute(sub_rows[s])
        for s in range(n_live, n_sub):
            for o in o_refs:
                o[sub_rows[s], :] = jnp.zeros((sub, o.shape[1]), o.dtype)

    if not ragged:
        tile_body(n_sub)
    else:
        live = (tr_ref[i] + (sub - 1)) // sub
        for n_live in range(n_sub + 1):
            pl.when(live == n_live)(functools.partial(tile_body, n_live))


def grouped_matmul(x, weights, te, tr, *, tm, sub, tn, n_col_tiles, extras, out_dtypes,
                   out_cols, epilogue, name, ragged=False):
    rows, k = x.shape
    n_w, n_extra, n_out = len(weights), len(extras), len(out_dtypes)
    later = jnp.where(te[None, :] > te[:, None], te[None, :], jnp.iinfo(jnp.int32).max)
    nx = jnp.min(later, axis=1)
    nx = jnp.where(nx == jnp.iinfo(jnp.int32).max, -1, nx).astype(I32)
    tile_id = jnp.arange(rows // tm, dtype=I32)
    has_rows_before = jnp.logical_and(tile_id[None, :] <= tile_id[:, None], tr[None, :] > 0)
    ti = jnp.max(jnp.where(has_rows_before, tile_id[None, :], 0), axis=1).astype(I32)
    new_group = jnp.concatenate([jnp.zeros((1,), I32), (te[1:] != te[:-1]).astype(I32)])
    ordinal = jnp.sum(jnp.where(tile_id[None, :] <= tile_id[:, None], new_group[None, :], 0), axis=1)
    go = jnp.concatenate([ordinal, ordinal[-1:] + 1]).astype(I32)

    in_specs = [pl.BlockSpec((tm, k), lambda j, i, te, tr, nx, ti, go: (ti[i], 0))]
    in_specs += [pl.BlockSpec(memory_space=pl.ANY) for _ in weights]
    for _, bshape, imap in extras:
        in_specs.append(pl.BlockSpec(
            bshape, lambda j, i, te, tr, nx, ti, go, imap=imap: imap(j, ti[i])))
    out_specs = [pl.BlockSpec((tm, tn), lambda j, i, *_: (i, j)) for _ in out_dtypes]
    out_shape = [jax.ShapeDtypeStruct((rows, out_cols), dt) for dt in out_dtypes]

    vmem = 2 * _nbytes((tm, k), x.dtype)
    vmem += 2 * n_w * _nbytes((k, tn), F32)
    vmem += sum(2 * _nbytes(b, a.dtype) for a, b, _ in extras)
    vmem += sum(2 * _nbytes((tm, tn), dt) for dt in out_dtypes)
    vmem += (n_w + 3) * _nbytes((tm, tn), F32)
    vmem += _nbytes((tm, k), x.dtype) + 4 * 1024 * 1024

    return pl.pallas_call(
        functools.partial(_gmm_kernel, n_w=n_w, n_extra=n_extra, n_out=n_out,
                          col_offs=tuple(off for _, off in weights), tn=tn, sub=sub,
                          ragged=ragged, epilogue=epilogue),
        grid_spec=pltpu.PrefetchScalarGridSpec(
            num_scalar_prefetch=5,
            grid=(n_col_tiles, rows // tm),
            in_specs=in_specs,
            out_specs=out_specs,
            scratch_shapes=[pltpu.VMEM((2, n_w, k, tn), F32),
                            pltpu.SemaphoreType.DMA((2, n_w))]),
        out_shape=out_shape,
        compiler_params=_params(vmem, 2),
        name=name,
    )(te, tr, nx, ti, go, x, *[w for w, _ in weights], *[a for a, _, _ in extras])


def _dense_tiles(rows, tm, group):
    n = rows // tm
    return jnp.full((n,), group, I32), jnp.full((n,), tm, I32)


def _epi_swiglu(j, rows, accs, e_refs, o_refs):
    g, u = accs
    o_refs[0][rows, :] = (g * jax.nn.sigmoid(g) * u).astype(BF16)


def _epi_residual(j, rows, accs, e_refs, o_refs):
    o_refs[0][rows, :] = e_refs[0][rows, :] + accs[0]


def _epi_plain(j, rows, accs, e_refs, o_refs):
    o_refs[0][rows, :] = accs[0]


def _epi_conv_in(j, rows, accs, e_refs, o_refs):
    gate_b, gate_c, val = accs
    o_refs[0][rows, :] = gate_b.astype(BF16)
    o_refs[1][rows, :] = (gate_c * val).astype(BF16)


def _epi_qkv(j, rows, accs, e_refs, o_refs, *, heads_per_tile, n_rope_tiles):
    acc = accs[0]
    cos_ref, sin_ref, gain_ref = e_refs
    o = o_refs[0]
    cos = cos_ref[rows, :]
    sin = sin_ref[rows, :]
    is_qk = j < n_rope_tiles
    for hh in range(heads_per_tile):
        sl = slice(hh * HEAD_DIM, (hh + 1) * HEAD_DIM)
        a = acc[:, sl]
        y = _rms(a, gain_ref[:, sl])
        rot = pltpu.roll(y, HEAD_DIM // 2, axis=1)
        o[rows, sl] = jnp.where(is_qk, y * cos + rot * sin, a).astype(BF16)


def _conv_kernel(gb_ref, u_ref, up_ref, un_ref, w_ref, o_ref, *, tm, seq):
    i = pl.program_id(0)
    u = u_ref[...].astype(F32)
    w = w_ref[...]
    row = lax.broadcasted_iota(I32, u.shape, 0)
    starts_seq = (i * tm) % seq == 0
    ends_seq = ((i + 1) * tm) % seq == 0
    prev_row = up_ref[...].astype(F32)[BF16_SUBLANES - 1:BF16_SUBLANES, :]
    next_row = un_ref[...].astype(F32)[0:1, :]
    prev_row = jnp.where(starts_seq, 0.0, prev_row)
    next_row = jnp.where(ends_seq, 0.0, next_row)
    u_prev = jnp.where(row == 0, prev_row, pltpu.roll(u, 1, axis=0))
    u_next = jnp.where(row == tm - 1, next_row, pltpu.roll(u, tm - 1, axis=0))
    conv = w[0:1, :] * u_prev + w[1:2, :] * u + w[2:3, :] * u_next
    o_ref[...] = (gb_ref[...].astype(F32) * conv).astype(BF16)


def conv_gate(gb, u, w_conv, seq, tm=512):
    t, d = u.shape
    hb = tm // BF16_SUBLANES
    n_halo = t // BF16_SUBLANES
    return pl.pallas_call(
        functools.partial(_conv_kernel, tm=tm, seq=seq),
        grid=(t // tm,),
        in_specs=[pl.BlockSpec((tm, d), lambda i: (i, 0)),
                  pl.BlockSpec((tm, d), lambda i: (i, 0)),
                  pl.BlockSpec((BF16_SUBLANES, d), lambda i: (jnp.maximum(i * hb - 1, 0), 0)),
                  pl.BlockSpec((BF16_SUBLANES, d),
                               lambda i: (jnp.minimum((i + 1) * hb, n_halo - 1), 0)),
                  pl.BlockSpec((3, d), lambda i: (0, 0))],
        out_specs=pl.BlockSpec((tm, d), lambda i: (i, 0)),
        out_shape=jax.ShapeDtypeStruct((t, d), BF16),
        compiler_params=_params(12 * tm * d * 4, 1),
        name="conv_gate",
    )(gb, u, u, u, w_conv)


def _rope_kernel(pos_ref, invf_ref, sign_ref, cos_ref, sin_ref):
    ang = pos_ref[...].astype(F32) * invf_ref[...]
    cos_ref[...] = jnp.cos(ang)
    sin_ref[...] = jnp.sin(ang) * sign_ref[...]


def rope_tables(positions, tm=512):
    t = positions.shape[0]
    half = HEAD_DIM // 2
    inv_freq = ROPE_THETA ** (-jnp.arange(0, half, dtype=F32) / half)
    invf = jnp.concatenate([inv_freq, inv_freq]).reshape(1, HEAD_DIM)
    sign = jnp.concatenate([-jnp.ones((half,), F32), jnp.ones((half,), F32)]).reshape(1, HEAD_DIM)
    return pl.pallas_call(
        _rope_kernel,
        grid=(t // tm,),
        in_specs=[pl.BlockSpec((tm, 1), lambda i: (i, 0)),
                  pl.BlockSpec((1, HEAD_DIM), lambda i: (0, 0)),
                  pl.BlockSpec((1, HEAD_DIM), lambda i: (0, 0))],
        out_specs=[pl.BlockSpec((tm, HEAD_DIM), lambda i: (i, 0))] * 2,
        out_shape=[jax.ShapeDtypeStruct((t, HEAD_DIM), F32)] * 2,
        compiler_params=_params(16 * tm * HEAD_DIM * 4, 1),
        name="rope_tables",
    )(positions.reshape(t, 1), invf, sign)


ATTN_ROWS = Q_PER_KV * ATTN_BLOCK
ATTN_KEYS = 3 * ATTN_BLOCK
ATTN_UNROLL = 4


def _attn_kernel(q_ref, k_ref, v_ref, sink_ref, o_ref, bias_ref):
    seq = q_ref.shape[0]
    sink = jnp.broadcast_to(sink_ref[...], (ATTN_ROWS, HEAD_DIM))
    q_off = lax.broadcasted_iota(I32, (ATTN_BLOCK, ATTN_KEYS), 0)
    k_off = lax.broadcasted_iota(I32, (ATTN_BLOCK, ATTN_KEYS), 1)
    for c in range(3):
        inside = jnp.abs(k_off - q_off - c * ATTN_BLOCK) <= WINDOW
        bias_ref[c] = jnp.where(inside, 0.0, MASK_VALUE)
    ones = jnp.ones((ATTN_KEYS, HEAD_DIM), BF16)

    def block(n):
        q0 = pl.multiple_of(n * ATTN_BLOCK, ATTN_BLOCK)
        k0 = pl.multiple_of(jnp.clip((n - 1) * ATTN_BLOCK, 0, seq - ATTN_KEYS), ATTN_BLOCK)
        qb = q_ref[pl.ds(q0, ATTN_BLOCK), :]
        qs = jnp.concatenate(
            [qb[:, g * HEAD_DIM:(g + 1) * HEAD_DIM] for g in range(Q_PER_KV)], axis=0)
        kw = k_ref[pl.ds(k0, ATTN_KEYS), :]
        v_ext = jnp.concatenate([v_ref[pl.ds(k0, ATTN_KEYS), :], ones], axis=1)
        s_all = lax.dot_general(qs, kw, (((1,), (1,)), ((), ())), preferred_element_type=F32)
        bias = bias_ref[(q0 - k0) // ATTN_BLOCK]
        ms, ps = [], []
        for g in range(Q_PER_KV):
            rows = slice(g * ATTN_BLOCK, (g + 1) * ATTN_BLOCK)
            s = s_all[rows, :] + bias
            m = jnp.maximum(jnp.max(s, axis=-1, keepdims=True), sink[rows, :])
            p = jnp.exp(s - jnp.concatenate([m] * (ATTN_KEYS // HEAD_DIM), axis=1))
            ms.append(m)
            ps.append(p.astype(BF16))
        pv = jnp.dot(jnp.concatenate(ps, axis=0), v_ext, preferred_element_type=F32)
        for g in range(Q_PER_KV):
            rows = slice(g * ATTN_BLOCK, (g + 1) * ATTN_BLOCK)
            denom = pv[rows, HEAD_DIM:] + jnp.exp(sink[rows, :] - ms[g])
            o = pv[rows, :HEAD_DIM] / denom
            o_ref[pl.ds(q0, ATTN_BLOCK), g * HEAD_DIM:(g + 1) * HEAD_DIM] = o.astype(BF16)

    def body(n2, carry):
        for r in range(ATTN_UNROLL):
            block(n2 * ATTN_UNROLL + r)
        return carry

    lax.fori_loop(0, seq // (ATTN_BLOCK * ATTN_UNROLL), body, 0)


def window_attention(qkv, sink, batch, seq):
    t = qkv.shape[0]
    n_heads = N_KV_HEADS * Q_PER_KV
    gw = Q_PER_KV * HEAD_DIM
    sink_col = jnp.repeat(sink.astype(F32).reshape(N_KV_HEADS, Q_PER_KV), ATTN_BLOCK, axis=1)
    sink_col = sink_col.reshape(N_KV_HEADS, ATTN_ROWS, 1)
    return pl.pallas_call(
        _attn_kernel,
        grid=(batch, N_KV_HEADS),
        in_specs=[pl.BlockSpec((seq, gw), lambda b, h: (b, h)),
                  pl.BlockSpec((seq, HEAD_DIM), lambda b, h: (b, n_heads + h)),
                  pl.BlockSpec((seq, HEAD_DIM), lambda b, h: (b, n_heads + N_KV_HEADS + h)),
                  pl.BlockSpec((None, ATTN_ROWS, 1), lambda b, h: (h, 0, 0))],
        out_specs=pl.BlockSpec((seq, gw), lambda b, h: (b, h)),
        out_shape=jax.ShapeDtypeStruct((t, n_heads * HEAD_DIM), BF16),
        scratch_shapes=[pltpu.VMEM((3, ATTN_BLOCK, ATTN_KEYS), F32)],
        compiler_params=_params(6 * seq * gw * 2 + 24 * 1024 * 1024, 2),
        name="window_attention",
    )(qkv, qkv, qkv, sink_col)


def _route_kernel(x_ref, g_ref, r_ref, h_ref, meta_ref, gate_ref, cnt_ref, carry_ref, *, tm):
    i = pl.program_id(0)

    @pl.when(i == 0)
    def _init():
        carry_ref[...] = jnp.zeros(carry_ref.shape, carry_ref.dtype)

    h = _rms(x_ref[...], g_ref[...])
    h_ref[...] = h.astype(BF16)
    r = r_ref[...]
    h_hi = h.astype(BF16)
    h_lo = (h - h_hi.astype(F32)).astype(BF16)
    r_hi = r.astype(BF16)
    r_lo = (r - r_hi.astype(F32)).astype(BF16)
    logits = (jnp.dot(h_hi, r_hi, preferred_element_type=F32)
              + (jnp.dot(h_hi, r_lo, preferred_element_type=F32)
                 + jnp.dot(h_lo, r_hi, preferred_element_type=F32)))
    lane = lax.broadcasted_iota(I32, logits.shape, 1)
    lane_f = lane.astype(F32)
    neg_inf = jnp.float32(-jnp.inf)
    logits = jnp.where(lane < N_EXPERTS, logits, neg_inf)
    v1 = jnp.max(logits, axis=-1, keepdims=True)
    i1 = jnp.min(jnp.where(logits == v1, lane_f, float(LANES)), axis=-1, keepdims=True)
    rest = jnp.where(lane_f == i1, neg_inf, logits)
    v2 = jnp.max(rest, axis=-1, keepdims=True)
    i2 = jnp.min(jnp.where(rest == v2, lane_f, float(LANES)), axis=-1, keepdims=True)
    e21 = jnp.exp(v2 - v1)
    g1 = 1.0 / (1.0 + e21)
    g2 = e21 / (1.0 + e21)

    sel1 = lane_f == i1
    sel2 = lane_f == i2
    onehot = jnp.logical_or(sel1, sel2)
    r_io = lax.broadcasted_iota(I32, (tm, tm), 0)
    c_io = lax.broadcasted_iota(I32, (tm, tm), 1)
    strict_lower = (c_io < r_io).astype(BF16)
    carry = carry_ref[...]
    before = jnp.dot(strict_lower, onehot.astype(BF16), preferred_element_type=F32) + carry
    rank1 = jnp.sum(jnp.where(sel1, before, 0.0), axis=-1, keepdims=True)
    rank2 = jnp.sum(jnp.where(sel2, before, 0.0), axis=-1, keepdims=True)
    carry = carry + jnp.sum(onehot.astype(F32), axis=0, keepdims=True)
    carry_ref[...] = carry
    cnt_ref[...] = carry.astype(I32)

    meta = jnp.where(lane == 0, i1, jnp.where(lane == 1, i2,
           jnp.where(lane == 2, rank1, jnp.where(lane == 3, rank2, 0.0))))
    meta_ref[...] = meta.astype(I32)
    gate_ref[...] = jnp.where(lane == 0, g1, jnp.where(lane == 1, g2, 0.0))


def moe_route(x, g, w_router, tm=512):
    t, d = x.shape
    r_pad = jnp.zeros((d, LANES), F32).at[:, :N_EXPERTS].set(w_router)
    return pl.pallas_call(
        functools.partial(_route_kernel, tm=tm),
        grid=(t // tm,),
        in_specs=[pl.BlockSpec((tm, d), lambda i: (i, 0)),
                  pl.BlockSpec((1, d), lambda i: (0, 0)),
                  pl.BlockSpec((d, LANES), lambda i: (0, 0))],
        out_specs=[pl.BlockSpec((tm, d), lambda i: (i, 0)),
                   pl.BlockSpec((tm, LANES), lambda i: (i, 0)),
                   pl.BlockSpec((tm, LANES), lambda i: (i, 0)),
                   pl.BlockSpec((1, LANES), lambda i: (0, 0))],
        out_shape=[jax.ShapeDtypeStruct((t, d), BF16),
                   jax.ShapeDtypeStruct((t, LANES), I32),
                   jax.ShapeDtypeStruct((t, LANES), F32),
                   jax.ShapeDtypeStruct((1, LANES), I32)],
        scratch_shapes=[pltpu.VMEM((1, LANES), F32)],
        compiler_params=_params(10 * tm * d * 4 + 8 * 1024 * 1024, 1),
        name="moe_route",
    )(x, g.reshape(1, d), r_pad)


DISPATCH_TM = 512
DISPATCH_UNROLL = 8


def _dispatch_kernel(pos_ref, h_ref, zeros_hbm, xs_hbm, sem):
    del zeros_hbm
    tm = h_ref.shape[0]
    base = pl.program_id(0) * tm

    def body(r, c):
        for k in range(2):
            pltpu.make_async_copy(
                h_ref.at[r], xs_hbm.at[pos_ref[2 * (base + r) + k]], sem).start(priority=k)
        return c
    lax.fori_loop(0, tm, body, 0, unroll=DISPATCH_UNROLL)
    for k in range(2):
        pltpu.make_async_copy(h_ref, xs_hbm.at[pl.ds(0, tm)], sem).wait()


def moe_dispatch(h, pos_flat, n_slots):
    t, d = h.shape
    h3 = h.reshape(t, d // LANES, LANES)
    zeros = jnp.zeros((n_slots, d // LANES, LANES), BF16)
    tm = DISPATCH_TM
    xs = pl.pallas_call(
        _dispatch_kernel,
        grid_spec=pltpu.PrefetchScalarGridSpec(
            num_scalar_prefetch=1,
            grid=(t // tm,),
            in_specs=[pl.BlockSpec((tm, d // LANES, LANES), lambda i, pos: (i, 0, 0)),
                      pl.BlockSpec(memory_space=pl.ANY)],
            out_specs=pl.BlockSpec(memory_space=pl.ANY),
            scratch_shapes=[pltpu.SemaphoreType.DMA(())]),
        out_shape=jax.ShapeDtypeStruct(zeros.shape, BF16),
        input_output_aliases={2: 0},
        compiler_params=_params(4 * tm * d * 2 + 4 * 1024 * 1024, 1),
        name="moe_dispatch",
    )(pos_flat, h3, zeros)
    return xs.reshape(n_slots, d)


def _combine_kernel(pos_ref, x_ref, gate_ref, y_hbm, g_ref, *refs, tc, with_norm):
    if with_norm:
        xo_ref, ho_ref, buf, sem = refs
    else:
        xo_ref, buf, sem = refs
    i = pl.program_id(0)

    def issue(tile, slot):
        def body(r, c):
            t = tile * tc + r
            for k in range(2):
                pltpu.make_async_copy(
                    y_hbm.at[pl.ds(pos_ref[2 * t + k], 1), :],
                    buf.at[slot, pl.ds(k * tc + r, 1), :], sem.at[slot]).start(priority=k)
            return c
        lax.fori_loop(0, tc, body, 0, unroll=8)

    @pl.when(i == 0)
    def _prime():
        issue(0, 0)

    @pl.when(i + 1 < pl.num_programs(0))
    def _prefetch():
        issue(i + 1, (i + 1) % 2)

    slot = i % 2
    pltpu.make_async_copy(y_hbm.at[pl.ds(0, 2 * tc), :], buf.at[slot], sem.at[slot]).wait()
    gates = gate_ref[...]
    y = gates[:, 0:1] * buf[slot, 0:tc, :] + gates[:, 1:2] * buf[slot, tc:2 * tc, :]
    xn = x_ref[...] + y
    xo_ref[...] = xn
    if with_norm:
        ho_ref[...] = _rms(xn, g_ref[...]).astype(BF16)


def moe_combine(x, gates, y, pos_flat, g_next, tc=256):
    t, d = x.shape
    with_norm = g_next is not None
    g_arr = (g_next if with_norm else jnp.ones((d,), F32)).reshape(1, d)
    out_specs = [pl.BlockSpec((tc, d), lambda i, pos: (i, 0))]
    out_shape = [jax.ShapeDtypeStruct((t, d), F32)]
    if with_norm:
        out_specs.append(pl.BlockSpec((tc, d), lambda i, pos: (i, 0)))
        out_shape.append(jax.ShapeDtypeStruct((t, d), BF16))
    outs = pl.pallas_call(
        functools.partial(_combine_kernel, tc=tc, with_norm=with_norm),
        grid_spec=pltpu.PrefetchScalarGridSpec(
            num_scalar_prefetch=1,
            grid=(t // tc,),
            in_specs=[pl.BlockSpec((tc, d), lambda i, pos: (i, 0)),
                      pl.BlockSpec((tc, LANES), lambda i, pos: (i, 0)),
                      pl.BlockSpec(memory_space=pl.ANY),
                      pl.BlockSpec((1, d), lambda i, pos: (0, 0))],
            out_specs=out_specs,
            scratch_shapes=[pltpu.VMEM((2, 2 * tc, d), F32), pltpu.SemaphoreType.DMA((2,))]),
        out_shape=out_shape,
        compiler_params=_params(20 * tc * d * 4 + 4 * 1024 * 1024, 1),
        name="moe_combine",
    )(pos_flat, x, gates, y, g_arr)
    return outs if with_norm else (outs[0], None)


def moe_layer(x, g_ffn, w_router, w13, w2, layer, g_next):
    t, d = x.shape
    d_ff = w2.shape[1]
    tm = MOE_TM
    n_tiles = (2 * t) // tm + N_EXPERTS
    n_slots = n_tiles * tm

    h, meta, gates, counts = moe_route(x, g_ffn, w_router)
    cnt = counts[0, :N_EXPERTS]
    tiles_per = (cnt + tm - 1) // tm
    tile_end = jnp.cumsum(tiles_per)
    tile_start = tile_end - tiles_per
    row_start = tile_start * tm
    e1, e2, r1, r2 = meta[:, 0], meta[:, 1], meta[:, 2], meta[:, 3]
    pos = jnp.stack([row_start[e1] + r1, row_start[e2] + r2], axis=1).reshape(-1).astype(I32)
    tile_id = jnp.arange(n_tiles, dtype=I32)
    used = tile_end[-1]
    te = jnp.sum(jnp.minimum(tile_id, used - 1)[:, None] >= tile_end[None, :], axis=1).astype(I32)
    tr = jnp.clip(cnt[te] - (tile_id - tile_start[te]) * tm, 0, tm)
    tr = jnp.where(tile_id < used, tr, 0).astype(I32)
    te = te + layer * N_EXPERTS

    xs = moe_dispatch(h, pos, n_slots)

    tn13 = 1024
    (act,) = grouped_matmul(
        xs, [(w13, 0), (w13, d_ff // tn13)], te, tr, tm=tm, sub=MOE_SUB_W13, tn=tn13,
        n_col_tiles=d_ff // tn13, extras=[], out_dtypes=[BF16], out_cols=d_ff,
        epilogue=_epi_swiglu, name="moe_w13", ragged=True)
    tn2 = 512
    tm2 = MOE_TM_W2
    parts = tm // tm2
    te2 = jnp.repeat(te, parts)
    tr2 = jnp.clip(tr[:, None] - jnp.arange(parts, dtype=I32)[None, :] * tm2, 0, tm2).reshape(-1)
    (y,) = grouped_matmul(
        act, [(w2, 0)], te2, tr2, tm=tm2, sub=MOE_SUB_W2, tn=tn2, n_col_tiles=d // tn2, extras=[],
        out_dtypes=[F32], out_cols=d, epilogue=_epi_plain, name="moe_w2", ragged=True)
    return moe_combine(x, gates, y, pos, g_next)


def conv_layer(x, h, w_in, w_conv, w_out, layer, seq):
    t, d = x.shape
    tn = 512
    tm = 1024
    te, tr = _dense_tiles(t, tm, layer)
    nt = d // tn
    gb, u = grouped_matmul(
        h, [(w_in, 0), (w_in, nt), (w_in, 2 * nt)], te, tr, tm=tm, sub=tm // 2, tn=tn,
        n_col_tiles=nt, extras=[], out_dtypes=[BF16, BF16], out_cols=d,
        epilogue=_epi_conv_in, name="conv_in")
    z = conv_gate(gb, u, w_conv, seq)
    tn = 1024
    (x,) = grouped_matmul(
        z, [(w_out, 0)], te, tr, tm=tm, sub=tm // 2, tn=tn, n_col_tiles=d // tn,
        extras=[(x, (tm, tn), lambda j, i: (i, j))], out_dtypes=[F32], out_cols=d,
        epilogue=_epi_residual, name="conv_out")
    return x


def dense_ffn(x, h, w13, w2, layer):
    t, d = x.shape
    d_ff = w2.shape[1]
    tn = 1024
    tm = 1024
    te, tr = _dense_tiles(t, tm, layer)
    (act,) = grouped_matmul(
        h, [(w13, 0), (w13, d_ff // tn)], te, tr, tm=tm, sub=512, tn=tn,
        n_col_tiles=d_ff // tn, extras=[], out_dtypes=[BF16], out_cols=d_ff,
        epilogue=_epi_swiglu, name="ffn_w13")
    tn = 512
    tm = 512
    te, tr = _dense_tiles(t, tm, layer)
    (x,) = grouped_matmul(
        act, [(w2, 0)], te, tr, tm=tm, sub=tm // 2, tn=tn, n_col_tiles=d // tn,
        extras=[(x, (tm, tn), lambda j, i: (i, j))], out_dtypes=[F32], out_cols=d,
        epilogue=_epi_residual, name="ffn_w2")
    return x


def attention_layer(x, h, cos, sin, w_qkv, q_gain, k_gain, sink, w_out, layer, batch, seq):
    t, d = x.shape
    n_heads = N_KV_HEADS * Q_PER_KV
    tn = 512
    tm = 1024
    heads_per_tile = tn // HEAD_DIM
    qkv_dim = w_qkv.shape[2]
    n_rope_tiles = (n_heads + N_KV_HEADS) // heads_per_tile
    gain_cols = jnp.concatenate([
        jnp.tile(q_gain.astype(F32) * (1.0 / math.sqrt(HEAD_DIM)), n_heads),
        jnp.tile(k_gain.astype(F32), N_KV_HEADS),
        jnp.ones((N_KV_HEADS * HEAD_DIM,), F32)]).reshape(1, qkv_dim)
    te, tr = _dense_tiles(t, tm, layer)
    (qkv,) = grouped_matmul(
        h, [(w_qkv, 0)], te, tr, tm=tm, sub=256, tn=tn, n_col_tiles=qkv_dim // tn,
        extras=[(cos, (tm, HEAD_DIM), lambda j, i: (i, 0)),
                (sin, (tm, HEAD_DIM), lambda j, i: (i, 0)),
                (gain_cols, (1, tn), lambda j, i: (0, j))],
        out_dtypes=[BF16], out_cols=qkv_dim,
        epilogue=functools.partial(_epi_qkv, heads_per_tile=heads_per_tile,
                                   n_rope_tiles=n_rope_tiles),
        name="attn_qkv")
    o = window_attention(qkv, sink, batch, seq)
    tn = 1024
    (x,) = grouped_matmul(
        o, [(w_out, 0)], te, tr, tm=tm, sub=tm // 2, tn=tn, n_col_tiles=d // tn,
        extras=[(x, (tm, tn), lambda j, i: (i, j))], out_dtypes=[F32], out_cols=d,
        epilogue=_epi_residual, name="attn_out")
    return x


def kernel(x, positions, norm_mix, norm_ffn, conv_in, conv_w, conv_out, attn_qkv, q_norm, k_norm,
           attn_sink, attn_out, ffn_w13, ffn_w2, router, moe_w13, moe_w2):
    batch, seq, d = x.shape
    depth = norm_mix.shape[0]
    t = batch * seq
    x = x.reshape(t, d)
    moe_w13 = moe_w13.reshape((-1,) + moe_w13.shape[2:])
    moe_w2 = moe_w2.reshape((-1,) + moe_w2.shape[2:])
    cos, sin = rope_tables(positions.reshape(t))
    h = rms_norm_bf16(x, norm_mix[0])
    for i in range(depth):
        j = i // 2
        if i % 2 == 0:
            x = conv_layer(x, h, conv_in, conv_w[j], conv_out, j, seq)
            h = rms_norm_bf16(x, norm_ffn[i])
            x = dense_ffn(x, h, ffn_w13, ffn_w2, j)
            h = rms_norm_bf16(x, norm_mix[i + 1])
        else:
            x = attention_layer(x, h, cos, sin, attn_qkv, q_norm[j], k_norm[j],
                                attn_sink[j], attn_out, j, batch, seq)
            g_next = norm_mix[i + 1] if i + 1 < depth else None
            x, h = moe_layer(x, norm_ffn[i], router[j], moe_w13, moe_w2, j, g_next)
    return x.reshape(batch, seq, d)
```

```python
import functools
import math

import jax
import jax.numpy as jnp
from jax import lax
from jax.experimental import pallas as pl
from jax.experimental.pallas import tpu as pltpu

F32 = jnp.float32
BF16 = jnp.bfloat16
I32 = jnp.int32

HEAD_DIM = 128
N_KV_HEADS = 4
Q_PER_KV = 4
WINDOW = 128
ATTN_BLOCK = 128
ROPE_THETA = 10000.0
N_EXPERTS = 8
RMS_EPS = 1e-6
MASK_VALUE = -1e30

LANES = 128
BF16_SUBLANES = 16
VMEM_BUDGET = 60000 * 1024

MOE_TM = 1024
MOE_SUB_W13 = 256
MOE_SUB_W2 = 128
MOE_TM_W2 = 512


def _params(vmem_bytes, n_grid):
    return pltpu.CompilerParams(
        dimension_semantics=("arbitrary",) * n_grid,
        vmem_limit_bytes=int(min(vmem_bytes, VMEM_BUDGET)))


def _nbytes(shape, dtype):
    return math.prod(shape) * jnp.dtype(dtype).itemsize


def _rms(x, g):
    ms = jnp.mean(x * x, axis=-1, keepdims=True)
    return x * lax.rsqrt(ms + RMS_EPS) * g


def _norm_kernel(x_ref, g_ref, o_ref):
    o_ref[...] = _rms(x_ref[...], g_ref[...]).astype(o_ref.dtype)


def rms_norm_bf16(x, g, tm=512):
    t, d = x.shape
    return pl.pallas_call(
        _norm_kernel,
        grid=(t // tm,),
        in_specs=[pl.BlockSpec((tm, d), lambda i: (i, 0)),
                  pl.BlockSpec((1, d), lambda i: (0, 0))],
        out_specs=pl.BlockSpec((tm, d), lambda i: (i, 0)),
        out_shape=jax.ShapeDtypeStruct((t, d), BF16),
        compiler_params=_params(6 * tm * d * 4, 1),
        name="rms_norm",
    )(x, g.reshape(1, d))


def _gmm_kernel(te_ref, tr_ref, nx_ref, ti_ref, go_ref, x_ref, *refs, n_w, n_extra, n_out,
                col_offs, tn, sub, ragged, epilogue):
    del ti_ref
    w_refs = refs[:n_w]
    e_refs = refs[n_w:n_w + n_extra]
    o_refs = refs[n_w + n_extra:n_w + n_extra + n_out]
    stage, sem = refs[n_w + n_extra + n_out:]
    j = pl.program_id(0)
    i = pl.program_id(1)
    n_j = pl.num_programs(0)
    n_i = pl.num_programs(1)
    tm = x_ref.shape[0]
    cur = te_ref[i]
    first_of_group = jnp.logical_or(i == 0, cur != te_ref[jnp.maximum(i - 1, 0)])
    slot = (j * go_ref[n_i] + go_ref[i]) % 2

    def weight_copy(v, group, jj, to_slot):
        col = pl.multiple_of((jj + col_offs[v]) * tn, tn)
        return pltpu.make_async_copy(
            w_refs[v].at[group, :, pl.ds(col, tn)], stage.at[to_slot, v], sem.at[to_slot, v])

    @pl.when(jnp.logical_and(j == 0, i == 0))
    def _prime():
        for v in range(n_w):
            weight_copy(v, cur, j, slot).start()

    @pl.when(first_of_group)
    def _switch_weights():
        for v in range(n_w):
            weight_copy(v, cur, j, slot).wait()
        nxt = nx_ref[i]

        @pl.when(nxt >= 0)
        def _():
            for v in range(n_w):
                weight_copy(v, nxt, j, 1 - slot).start()

        @pl.when(jnp.logical_and(nxt < 0, j + 1 < n_j))
        def _():
            for v in range(n_w):
                weight_copy(v, te_ref[0], j + 1, 1 - slot).start()

    def compute(rows):
        xb = x_ref[rows, :]
        accs = [lax.dot_general(xb, stage[slot, v], (((1,), (0,)), ((), ())),
                                preferred_element_type=F32) for v in range(n_w)]
        epilogue(j, rows, accs, e_refs, o_refs)

    n_sub = tm // sub
    sub_rows = [slice(s * sub, (s + 1) * sub) for s in range(n_sub)]

    def tile_body(n_live):
        for s in range(n_live):
            compute(sub_rows[s])
        for s in range(n_live, n_sub):
            for o in o_refs:
                o[sub_rows[s], :] = jnp.zeros((sub, o.shape[1]), o.dtype)

    if not ragged:
        tile_body(n_sub)
    else:
        live = (tr_ref[i] + (sub - 1)) // sub
        for n_live in range(n_sub + 1):
            pl.when(live == n_live)(functools.partial(tile_body, n_live))


def grouped_matmul(x, weights, te, tr, *, tm, sub, tn, n_col_tiles, extras, out_dtypes,
                   out_cols, epilogue, name, ragged=False):
    rows, k = x.shape
    n_w, n_extra, n_out = len(weights), len(extras), len(out_dtypes)
    later = jnp.where(te[None, :] > te[:, None], te[None, :], jnp.iinfo(jnp.int32).max)
    nx = jnp.min(later, axis=1)
    nx = jnp.where(nx == jnp.iinfo(jnp.int32).max, -1, nx).astype(I32)
    tile_id = jnp.arange(rows // tm, dtype=I32)
    has_rows_before = jnp.logical_and(tile_id[None, :] <= tile_id[:, None], tr[None, :] > 0)
    ti = jnp.max(jnp.where(has_rows_before, tile_id[None, :], 0), axis=1).astype(I32)
    new_group = jnp.concatenate([jnp.zeros((1,), I32), (te[1:] != te[:-1]).astype(I32)])
    ordinal = jnp.sum(jnp.where(tile_id[None, :] <= tile_id[:, None], new_group[None, :], 0), axis=1)
    go = jnp.concatenate([ordinal, ordinal[-1:] + 1]).astype(I32)

    in_specs = [pl.BlockSpec((tm, k), lambda j, i, te, tr, nx, ti, go: (ti[i], 0))]
    in_specs += [pl.BlockSpec(memory_space=pl.ANY) for _ in weights]
    for _, bshape, imap in extras:
        in_specs.append(pl.BlockSpec(
            bshape, lambda j, i, te, tr, nx, ti, go, imap=imap: imap(j, ti[i])))
    out_specs = [pl.BlockSpec((tm, tn), lambda j, i, *_: (i, j)) for _ in out_dtypes]
    out_shape = [jax.ShapeDtypeStruct((rows, out_cols), dt) for dt in out_dtypes]

    vmem = 2 * _nbytes((tm, k), x.dtype)
    vmem += 2 * n_w * _nbytes((k, tn), F32)
    vmem += sum(2 * _nbytes(b, a.dtype) for a, b, _ in extras)
    vmem += sum(2 * _nbytes((tm, tn), dt) for dt in out_dtypes)
    vmem += (n_w + 3) * _nbytes((tm, tn), F32)
    vmem += _nbytes((tm, k), x.dtype) + 4 * 1024 * 1024

    return pl.pallas_call(
        functools.partial(_gmm_kernel, n_w=n_w, n_extra=n_extra, n_out=n_out,
                          col_offs=tuple(off for _, off in weights), tn=tn, sub=sub,
                          ragged=ragged, epilogue=epilogue),
        grid_spec=pltpu.PrefetchScalarGridSpec(
            num_scalar_prefetch=5,
            grid=(n_col_tiles, rows // tm),
            in_specs=in_specs,
            out_specs=out_specs,
            scratch_shapes=[pltpu.VMEM((2, n_w, k, tn), F32),
                            pltpu.SemaphoreType.DMA((2, n_w))]),
        out_shape=out_shape,
        compiler_params=_params(vmem, 2),
        name=name,
    )(te, tr, nx, ti, go, x, *[w for w, _ in weights], *[a for a, _, _ in extras])


def _dense_tiles(rows, tm, group):
    n = rows // tm
    return jnp.full((n,), group, I32), jnp.full((n,), tm, I32)


def _epi_swiglu(j, rows, accs, e_refs, o_refs):
    g, u = accs
    o_refs[0][rows, :] = (g * jax.nn.sigmoid(g) * u).astype(BF16)


def _epi_residual(j, rows, accs, e_refs, o_refs):
    o_refs[0][rows, :] = e_refs[0][rows, :] + accs[0]


def _epi_plain(j, rows, accs, e_refs, o_refs):
    o_refs[0][rows, :] = accs[0]


def _epi_conv_in(j, rows, accs, e_refs, o_refs):
    gate_b, gate_c, val = accs
    o_refs[0][rows, :] = gate_b.astype(BF16)
    o_refs[1][rows, :] = (gate_c * val).astype(BF16)


def _epi_qkv(j, rows, accs, e_refs, o_refs, *, heads_per_tile, n_rope_tiles):
    acc = accs[0]
    cos_ref, sin_ref, gain_ref = e_refs
    o = o_refs[0]
    cos = cos_ref[rows, :]
    sin = sin_ref[rows, :]
    is_qk = j < n_rope_tiles
    for hh in range(heads_per_tile):
        sl = slice(hh * HEAD_DIM, (hh + 1) * HEAD_DIM)
        a = acc[:, sl]
        y = _rms(a, gain_ref[:, sl])
        rot = pltpu.roll(y, HEAD_DIM // 2, axis=1)
        o[rows, sl] = jnp.where(is_qk, y * cos + rot * sin, a).astype(BF16)


def _conv_kernel(gb_ref, u_ref, up_ref, un_ref, w_ref, o_ref, *, tm, seq):
    i = pl.program_id(0)
    u = u_ref[...].astype(F32)
    w = w_ref[...]
    row = lax.broadcasted_iota(I32, u.shape, 0)
    starts_seq = (i * tm) % seq == 0
    ends_seq = ((i + 1) * tm) % seq == 0
    prev_row = up_ref[...].astype(F32)[BF16_SUBLANES - 1:BF16_SUBLANES, :]
    next_row = un_ref[...].astype(F32)[0:1, :]
    prev_row = jnp.where(starts_seq, 0.0, prev_row)
    next_row = jnp.where(ends_seq, 0.0, next_row)
    u_prev = jnp.where(row == 0, prev_row, pltpu.roll(u, 1, axis=0))
    u_next = jnp.where(row == tm - 1, next_row, pltpu.roll(u, tm - 1, axis=0))
    conv = w[0:1, :] * u_prev + w[1:2, :] * u + w[2:3, :] * u_next
    o_ref[...] = (gb_ref[...].astype(F32) * conv).astype(BF16)


def conv_gate(gb, u, w_conv, seq, tm=512):
    t, d = u.shape
    hb = tm // BF16_SUBLANES
    n_halo = t // BF16_SUBLANES
    return pl.pallas_call(
        functools.partial(_conv_kernel, tm=tm, seq=seq),
        grid=(t // tm,),
        in_specs=[pl.BlockSpec((tm, d), lambda i: (i, 0)),
                  pl.BlockSpec((tm, d), lambda i: (i, 0)),
                  pl.BlockSpec((BF16_SUBLANES, d), lambda i: (jnp.maximum(i * hb - 1, 0), 0)),
                  pl.BlockSpec((BF16_SUBLANES, d),
                               lambda i: (jnp.minimum((i + 1) * hb, n_halo - 1), 0)),
                  pl.BlockSpec((3, d), lambda i: (0, 0))],
        out_specs=pl.BlockSpec((tm, d), lambda i: (i, 0)),
        out_shape=jax.ShapeDtypeStruct((t, d), BF16),
        compiler_params=_params(12 * tm * d * 4, 1),
        name="conv_gate",
    )(gb, u, u, u, w_conv)


def _rope_kernel(pos_ref, invf_ref, sign_ref, cos_ref, sin_ref):
    ang = pos_ref[...].astype(F32) * invf_ref[...]
    cos_ref[...] = jnp.cos(ang)
    sin_ref[...] = jnp.sin(ang) * sign_ref[...]


def rope_tables(positions, tm=512):
    t = positions.shape[0]
    half = HEAD_DIM // 2
    inv_freq = ROPE_THETA ** (-jnp.arange(0, half, dtype=F32) / half)
    invf = jnp.concatenate([inv_freq, inv_freq]).reshape(1, HEAD_DIM)
    sign = jnp.concatenate([-jnp.ones((half,), F32), jnp.ones((half,), F32)]).reshape(1, HEAD_DIM)
    return pl.pallas_call(
        _rope_kernel,
        grid=(t // tm,),
        in_specs=[pl.BlockSpec((tm, 1), lambda i: (i, 0)),
                  pl.BlockSpec((1, HEAD_DIM), lambda i: (0, 0)),
                  pl.BlockSpec((1, HEAD_DIM), lambda i: (0, 0))],
        out_specs=[pl.BlockSpec((tm, HEAD_DIM), lambda i: (i, 0))] * 2,
        out_shape=[jax.ShapeDtypeStruct((t, HEAD_DIM), F32)] * 2,
        compiler_params=_params(16 * tm * HEAD_DIM * 4, 1),
        name="rope_tables",
    )(positions.reshape(t, 1), invf, sign)


ATTN_ROWS = Q_PER_KV * ATTN_BLOCK
ATTN_KEYS = 3 * ATTN_BLOCK
ATTN_UNROLL = 4


def _attn_kernel(q_ref, k_ref, v_ref, sink_ref, o_ref, bias_ref):
    seq = q_ref.shape[0]
    sink = jnp.broadcast_to(sink_ref[...], (ATTN_ROWS, HEAD_DIM))
    q_off = lax.broadcasted_iota(I32, (ATTN_BLOCK, ATTN_KEYS), 0)
    k_off = lax.broadcasted_iota(I32, (ATTN_BLOCK, ATTN_KEYS), 1)
    for c in range(3):
        inside = jnp.abs(k_off - q_off - c * ATTN_BLOCK) <= WINDOW
        bias_ref[c] = jnp.where(inside, 0.0, MASK_VALUE)
    ones = jnp.ones((ATTN_KEYS, HEAD_DIM), BF16)

    def block(n):
        q0 = pl.multiple_of(n * ATTN_BLOCK, ATTN_BLOCK)
        k0 = pl.multiple_of(jnp.clip((n - 1) * ATTN_BLOCK, 0, seq - ATTN_KEYS), ATTN_BLOCK)
        qb = q_ref[pl.ds(q0, ATTN_BLOCK), :]
        qs = jnp.concatenate(
            [qb[:, g * HEAD_DIM:(g + 1) * HEAD_DIM] for g in range(Q_PER_KV)], axis=0)
        kw = k_ref[pl.ds(k0, ATTN_KEYS), :]
        v_ext = jnp.concatenate([v_ref[pl.ds(k0, ATTN_KEYS), :], ones], axis=1)
        s_all = lax.dot_general(qs, kw, (((1,), (1,)), ((), ())), preferred_element_type=F32)
        bias = bias_ref[(q0 - k0) // ATTN_BLOCK]
        ms, ps = [], []
        for g in range(Q_PER_KV):
            rows = slice(g * ATTN_BLOCK, (g + 1) * ATTN_BLOCK)
            s = s_all[rows, :] + bias
            m = jnp.maximum(jnp.max(s, axis=-1, keepdims=True), sink[rows, :])
            p = jnp.exp(s - jnp.concatenate([m] * (ATTN_KEYS // HEAD_DIM), axis=1))
            ms.append(m)
            ps.append(p.astype(BF16))
        pv = jnp.dot(jnp.concatenate(ps, axis=0), v_ext, preferred_element_type=F32)
        for g in range(Q_PER_KV):
            rows = slice(g * ATTN_BLOCK, (g + 1) * ATTN_BLOCK)
            denom = pv[rows, HEAD_DIM:] + jnp.exp(sink[rows, :] - ms[g])
            o = pv[rows, :HEAD_DIM] / denom
            o_ref[pl.ds(q0, ATTN_BLOCK), g * HEAD_DIM:(g + 1) * HEAD_DIM] = o.astype(BF16)

    def body(n2, carry):
        for r in range(ATTN_UNROLL):
            block(n2 * ATTN_UNROLL + r)
        return carry

    lax.fori_loop(0, seq // (ATTN_BLOCK * ATTN_UNROLL), body, 0)


def window_attention(qkv, sink, batch, seq):
    t = qkv.shape[0]
    n_heads = N_KV_HEADS * Q_PER_KV
    gw = Q_PER_KV * HEAD_DIM
    sink_col = jnp.repeat(sink.astype(F32).reshape(N_KV_HEADS, Q_PER_KV), ATTN_BLOCK, axis=1)
    sink_col = sink_col.reshape(N_KV_HEADS, ATTN_ROWS, 1)
    return pl.pallas_call(
        _attn_kernel,
        grid=(batch, N_KV_HEADS),
        in_specs=[pl.BlockSpec((seq, gw), lambda b, h: (b, h)),
                  pl.BlockSpec((seq, HEAD_DIM), lambda b, h: (b, n_heads + h)),
                  pl.BlockSpec((seq, HEAD_DIM), lambda b, h: (b, n_heads + N_KV_HEADS + h)),
                  pl.BlockSpec((None, ATTN_ROWS, 1), lambda b, h: (h, 0, 0))],
        out_specs=pl.BlockSpec((seq, gw), lambda b, h: (b, h)),
        out_shape=jax.ShapeDtypeStruct((t, n_heads * HEAD_DIM), BF16),
        scratch_shapes=[pltpu.VMEM((3, ATTN_BLOCK, ATTN_KEYS), F32)],
        compiler_params=_params(6 * seq * gw * 2 + 24 * 1024 * 1024, 2),
        name="window_attention",
    )(qkv, qkv, qkv, sink_col)


def _route_kernel(x_ref, g_ref, r_ref, h_ref, meta_ref, gate_ref, cnt_ref, carry_ref, *, tm):
    i = pl.program_id(0)

    @pl.when(i == 0)
    def _init():
        carry_ref[...] = jnp.zeros(carry_ref.shape, carry_ref.dtype)

    h = _rms(x_ref[...], g_ref[...])
    h_ref[...] = h.astype(BF16)
    r = r_ref[...]
    h_hi = h.astype(BF16)
    h_lo = (h - h_hi.astype(F32)).astype(BF16)
    r_hi = r.astype(BF16)
    r_lo = (r - r_hi.astype(F32)).astype(BF16)
    logits = (jnp.dot(h_hi, r_hi, preferred_element_type=F32)
              + (jnp.dot(h_hi, r_lo, preferred_element_type=F32)
                 + jnp.dot(h_lo, r_hi, preferred_element_type=F32)))
    lane = lax.broadcasted_iota(I32, logits.shape, 1)
    lane_f = lane.astype(F32)
    neg_inf = jnp.float32(-jnp.inf)
    logits = jnp.where(lane < N_EXPERTS, logits, neg_inf)
    v1 = jnp.max(logits, axis=-1, keepdims=True)
    i1 = jnp.min(jnp.where(logits == v1, lane_f, float(LANES)), axis=-1, keepdims=True)
    rest = jnp.where(lane_f == i1, neg_inf, logits)
    v2 = jnp.max(rest, axis=-1, keepdims=True)
    i2 = jnp.min(jnp.where(rest == v2, lane_f, float(LANES)), axis=-1, keepdims=True)
    e21 = jnp.exp(v2 - v1)
    g1 = 1.0 / (1.0 + e21)
    g2 = e21 / (1.0 + e21)

    sel1 = lane_f == i1
    sel2 = lane_f == i2
    onehot = jnp.logical_or(sel1, sel2)
    r_io = lax.broadcasted_iota(I32, (tm, tm), 0)
    c_io = lax.broadcasted_iota(I32, (tm, tm), 1)
    strict_lower = (c_io < r_io).astype(BF16)
    carry = carry_ref[...]
    before = jnp.dot(strict_lower, onehot.astype(BF16), preferred_element_type=F32) + carry
    rank1 = jnp.sum(jnp.where(sel1, before, 0.0), axis=-1, keepdims=True)
    rank2 = jnp.sum(jnp.where(sel2, before, 0.0), axis=-1, keepdims=True)
    carry = carry + jnp.sum(onehot.astype(F32), axis=0, keepdims=True)
    carry_ref[...] = carry
    cnt_ref[...] = carry.astype(I32)

    meta = jnp.where(lane == 0, i1, jnp.where(lane == 1, i2,
           jnp.where(lane == 2, rank1, jnp.where(lane == 3, rank2, 0.0))))
    meta_ref[...] = meta.astype(I32)
    gate_ref[...] = jnp.where(lane == 0, g1, jnp.where(lane == 1, g2, 0.0))


def moe_route(x, g, w_router, tm=512):
    t, d = x.shape
    r_pad = jnp.zeros((d, LANES), F32).at[:, :N_EXPERTS].set(w_router)
    return pl.pallas_call(
        functools.partial(_route_kernel, tm=tm),
        grid=(t // tm,),
        in_specs=[pl.BlockSpec((tm, d), lambda i: (i, 0)),
                  pl.BlockSpec((1, d), lambda i: (0, 0)),
                  pl.BlockSpec((d, LANES), lambda i: (0, 0))],
        out_specs=[pl.BlockSpec((tm, d), lambda i: (i, 0)),
                   pl.BlockSpec((tm, LANES), lambda i: (i, 0)),
                   pl.BlockSpec((tm, LANES), lambda i: (i, 0)),
                   pl.BlockSpec((1, LANES), lambda i: (0, 0))],
        out_shape=[jax.ShapeDtypeStruct((t, d), BF16),
                   jax.ShapeDtypeStruct((t, LANES), I32),
                   jax.ShapeDtypeStruct((t, LANES), F32),
                   jax.ShapeDtypeStruct((1, LANES), I32)],
        scratch_shapes=[pltpu.VMEM((1, LANES), F32)],
        compiler_params=_params(10 * tm * d * 4 + 8 * 1024 * 1024, 1),
        name="moe_route",
    )(x, g.reshape(1, d), r_pad)


DISPATCH_TM = 512
DISPATCH_UNROLL = 8


def _dispatch_kernel(pos_ref, h_ref, zeros_hbm, xs_hbm, sem):
    del zeros_hbm
    tm = h_ref.shape[0]
    base = pl.program_id(0) * tm

    def body(r, c):
        for k in range(2):
            pltpu.make_async_copy(
                h_ref.at[r], xs_hbm.at[pos_ref[2 * (base + r) + k]], sem).start(priority=k)
        return c
    lax.fori_loop(0, tm, body, 0, unroll=DISPATCH_UNROLL)
    for k in range(2):
        pltpu.make_async_copy(h_ref, xs_hbm.at[pl.ds(0, tm)], sem).wait()


def moe_dispatch(h, pos_flat, n_slots):
    t, d = h.shape
    h3 = h.reshape(t, d // LANES, LANES)
    zeros = jnp.zeros((n_slots, d // LANES, LANES), BF16)
    tm = DISPATCH_TM
    xs = pl.pallas_call(
        _dispatch_kernel,
        grid_spec=pltpu.PrefetchScalarGridSpec(
            num_scalar_prefetch=1,
            grid=(t // tm,),
            in_specs=[pl.BlockSpec((tm, d // LANES, LANES), lambda i, pos: (i, 0, 0)),
                      pl.BlockSpec(memory_space=pl.ANY)],
            out_specs=pl.BlockSpec(memory_space=pl.ANY),
            scratch_shapes=[pltpu.SemaphoreType.DMA(())]),
        out_shape=jax.ShapeDtypeStruct(zeros.shape, BF16),
        input_output_aliases={2: 0},
        compiler_params=_params(4 * tm * d * 2 + 4 * 1024 * 1024, 1),
        name="moe_dispatch",
    )(pos_flat, h3, zeros)
    return xs.reshape(n_slots, d)


def _combine_kernel(pos_ref, x_ref, gate_ref, y_hbm, g_ref, *refs, tc, with_norm):
    if with_norm:
        xo_ref, ho_ref, buf, sem = refs
    else:
        xo_ref, buf, sem = refs
    i = pl.program_id(0)

    def issue(tile, slot):
        base = 2 * tile * tc
        for r in range(tc):
            for k in range(2):
                pltpu.make_async_copy(
                    y_hbm.at[pl.ds(pos_ref[base + 2 * r + k], 1), :],
                    buf.at[slot, pl.ds(k * tc + r, 1), :], sem.at[slot]).start(priority=k)

    @pl.when(i == 0)
    def _prime():
        issue(0, 0)

    for s in range(2):
        @pl.when(jnp.logical_and(i + 1 < pl.num_programs(0), (i + 1) % 2 == s))
        def _prefetch():
            issue(i + 1, s)

    slot = i % 2
    pltpu.make_async_copy(y_hbm.at[pl.ds(0, 2 * tc), :], buf.at[slot], sem.at[slot]).wait()
    gates = gate_ref[...]
    y = gates[:, 0:1] * buf[slot, 0:tc, :] + gates[:, 1:2] * buf[slot, tc:2 * tc, :]
    xn = x_ref[...] + y
    xo_ref[...] = xn
    if with_norm:
        ho_ref[...] = _rms(xn, g_ref[...]).astype(BF16)


def moe_combine(x, gates, y, pos_flat, g_next, tc=256):
    t, d = x.shape
    with_norm = g_next is not None
    g_arr = (g_next if with_norm else jnp.ones((d,), F32)).reshape(1, d)
    out_specs = [pl.BlockSpec((tc, d), lambda i, pos: (i, 0))]
    out_shape = [jax.ShapeDtypeStruct((t, d), F32)]
    if with_norm:
        out_specs.append(pl.BlockSpec((tc, d), lambda i, pos: (i, 0)))
        out_shape.append(jax.ShapeDtypeStruct((t, d), BF16))
    outs = pl.pallas_call(
        functools.partial(_combine_kernel, tc=tc, with_norm=with_norm),
        grid_spec=pltpu.PrefetchScalarGridSpec(
            num_scalar_prefetch=1,
            grid=(t // tc,),
            in_specs=[pl.BlockSpec((tc, d), lambda i, pos: (i, 0)),
                      pl.BlockSpec((tc, LANES), lambda i, pos: (i, 0)),
                      pl.BlockSpec(memory_space=pl.ANY),
                      pl.BlockSpec((1, d), lambda i, pos: (0, 0))],
            out_specs=out_specs,
            scratch_shapes=[pltpu.VMEM((2, 2 * tc, d), F32), pltpu.SemaphoreType.DMA((2,))]),
        out_shape=out_shape,
        compiler_params=_params(20 * tc * d * 4 + 4 * 1024 * 1024, 1),
        name="moe_combine",
    )(pos_flat, x, gates, y, g_arr)
    return outs if with_norm else (outs[0], None)


def moe_layer(x, g_ffn, w_router, w13, w2, layer, g_next):
    t, d = x.shape
    d_ff = w2.shape[1]
    tm = MOE_TM
    n_tiles = (2 * t) // tm + N_EXPERTS
    n_slots = n_tiles * tm

    h, meta, gates, counts = moe_route(x, g_ffn, w_router)
    cnt = counts[0, :N_EXPERTS]
    tiles_per = (cnt + tm - 1) // tm
    tile_end = jnp.cumsum(tiles_per)
    tile_start = tile_end - tiles_per
    row_start = tile_start * tm
    e1, e2, r1, r2 = meta[:, 0], meta[:, 1], meta[:, 2], meta[:, 3]
    pos = jnp.stack([row_start[e1] + r1, row_start[e2] + r2], axis=1).reshape(-1).astype(I32)
    tile_id = jnp.arange(n_tiles, dtype=I32)
    used = tile_end[-1]
    te = jnp.sum(jnp.minimum(tile_id, used - 1)[:, None] >= tile_end[None, :], axis=1).astype(I32)
    tr = jnp.clip(cnt[te] - (tile_id - tile_start[te]) * tm, 0, tm)
    tr = jnp.where(tile_id < used, tr, 0).astype(I32)
    te = te + layer * N_EXPERTS

    xs = moe_dispatch(h, pos, n_slots)

    tn13 = 1024
    (act,) = grouped_matmul(
        xs, [(w13, 0), (w13, d_ff // tn13)], te, tr, tm=tm, sub=MOE_SUB_W13, tn=tn13,
        n_col_tiles=d_ff // tn13, extras=[], out_dtypes=[BF16], out_cols=d_ff,
        epilogue=_epi_swiglu, name="moe_w13", ragged=True)
    tn2 = 512
    tm2 = MOE_TM_W2
    parts = tm // tm2
    te2 = jnp.repeat(te, parts)
    tr2 = jnp.clip(tr[:, None] - jnp.arange(parts, dtype=I32)[None, :] * tm2, 0, tm2).reshape(-1)
    (y,) = grouped_matmul(
        act, [(w2, 0)], te2, tr2, tm=tm2, sub=MOE_SUB_W2, tn=tn2, n_col_tiles=d // tn2, extras=[],
        out_dtypes=[F32], out_cols=d, epilogue=_epi_plain, name="moe_w2", ragged=True)
    return moe_combine(x, gates, y, pos, g_next)


def conv_layer(x, h, w_in, w_conv, w_out, layer, seq):
    t, d = x.shape
    tn = 512
    tm = 1024
    te, tr = _dense_tiles(t, tm, layer)
    nt = d // tn
    gb, u = grouped_matmul(
        h, [(w_in, 0), (w_in, nt), (w_in, 2 * nt)], te, tr, tm=tm, sub=tm // 2, tn=tn,
        n_col_tiles=nt, extras=[], out_dtypes=[BF16, BF16], out_cols=d,
        epilogue=_epi_conv_in, name="conv_in")
    z = conv_gate(gb, u, w_conv, seq)
    tn = 1024
    (x,) = grouped_matmul(
        z, [(w_out, 0)], te, tr, tm=tm, sub=tm // 2, tn=tn, n_col_tiles=d // tn,
        extras=[(x, (tm, tn), lambda j, i: (i, j))], out_dtypes=[F32], out_cols=d,
        epilogue=_epi_residual, name="conv_out")
    return x


def dense_ffn(x, h, w13, w2, layer):
    t, d = x.shape
    d_ff = w2.shape[1]
    tn = 1024
    tm = 1024
    te, tr = _dense_tiles(t, tm, layer)
    (act,) = grouped_matmul(
        h, [(w13, 0), (w13, d_ff // tn)], te, tr, tm=tm, sub=512, tn=tn,
        n_col_tiles=d_ff // tn, extras=[], out_dtypes=[BF16], out_cols=d_ff,
        epilogue=_epi_swiglu, name="ffn_w13")
    tn = 512
    tm = 512
    te, tr = _dense_tiles(t, tm, layer)
    (x,) = grouped_matmul(
        act, [(w2, 0)], te, tr, tm=tm, sub=tm // 2, tn=tn, n_col_tiles=d // tn,
        extras=[(x, (tm, tn), lambda j, i: (i, j))], out_dtypes=[F32], out_cols=d,
        epilogue=_epi_residual, name="ffn_w2")
    return x


def attention_layer(x, h, cos, sin, w_qkv, q_gain, k_gain, sink, w_out, layer, batch, seq):
    t, d = x.shape
    n_heads = N_KV_HEADS * Q_PER_KV
    tn = 512
    tm = 1024
    heads_per_tile = tn // HEAD_DIM
    qkv_dim = w_qkv.shape[2]
    n_rope_tiles = (n_heads + N_KV_HEADS) // heads_per_tile
    gain_cols = jnp.concatenate([
        jnp.tile(q_gain.astype(F32) * (1.0 / math.sqrt(HEAD_DIM)), n_heads),
        jnp.tile(k_gain.astype(F32), N_KV_HEADS),
        jnp.ones((N_KV_HEADS * HEAD_DIM,), F32)]).reshape(1, qkv_dim)
    te, tr = _dense_tiles(t, tm, layer)
    (qkv,) = grouped_matmul(
        h, [(w_qkv, 0)], te, tr, tm=tm, sub=256, tn=tn, n_col_tiles=qkv_dim // tn,
        extras=[(cos, (tm, HEAD_DIM), lambda j, i: (i, 0)),
                (sin, (tm, HEAD_DIM), lambda j, i: (i, 0)),
                (gain_cols, (1, tn), lambda j, i: (0, j))],
        out_dtypes=[BF16], out_cols=qkv_dim,
        epilogue=functools.partial(_epi_qkv, heads_per_tile=heads_per_tile,
                                   n_rope_tiles=n_rope_tiles),
        name="attn_qkv")
    o = window_attention(qkv, sink, batch, seq)
    tn = 1024
    (x,) = grouped_matmul(
        o, [(w_out, 0)], te, tr, tm=tm, sub=tm // 2, tn=tn, n_col_tiles=d // tn,
        extras=[(x, (tm, tn), lambda j, i: (i, j))], out_dtypes=[F32], out_cols=d,
        epilogue=_epi_residual, name="attn_out")
    return x


def kernel(x, positions, norm_mix, norm_ffn, conv_in, conv_w, conv_out, attn_qkv, q_norm, k_norm,
           attn_sink, attn_out, ffn_w13, ffn_w2, router, moe_w13, moe_w2):
    batch, seq, d = x.shape
    depth = norm_mix.shape[0]
    t = batch * seq
    x = x.reshape(t, d)
    moe_w13 = moe_w13.reshape((-1,) + moe_w13.shape[2:])
    moe_w2 = moe_w2.reshape((-1,) + moe_w2.shape[2:])
    cos, sin = rope_tables(positions.reshape(t))
    h = rms_norm_bf16(x, norm_mix[0])
    for i in range(depth):
        j = i // 2
        if i % 2 == 0:
            x = conv_layer(x, h, conv_in, conv_w[j], conv_out, j, seq)
            h = rms_norm_bf16(x, norm_ffn[i])
            x = dense_ffn(x, h, ffn_w13, ffn_w2, j)
            h = rms_norm_bf16(x, norm_mix[i + 1])
        else:
            x = attention_layer(x, h, cos, sin, attn_qkv, q_norm[j], k_norm[j],
                                attn_sink[j], attn_out, j, batch, seq)
            g_next = norm_mix[i + 1] if i + 1 < depth else None
            x, h = moe_layer(x, norm_ffn[i], router[j], moe_w13, moe_w2, j, g_next)
    return x.reshape(batch, seq, d)
```

```python
import functools
import math

import jax
import jax.numpy as jnp
from jax import lax
from jax.experimental import pallas as pl
from jax.experimental.pallas import tpu as pltpu

F32 = jnp.float32
BF16 = jnp.bfloat16
I32 = jnp.int32

HEAD_DIM = 128
N_KV_HEADS = 4
Q_PER_KV = 4
WINDOW = 128
ATTN_BLOCK = 128
ROPE_THETA = 10000.0
N_EXPERTS = 8
RMS_EPS = 1e-6
MASK_VALUE = -1e30

LANES = 128
BF16_SUBLANES = 16
VMEM_BUDGET = 60000 * 1024

MOE_TM = 1024
MOE_SUB_W13 = 256
MOE_SUB_W2 = 128
MOE_TM_W2 = 512


def _params(vmem_bytes, n_grid):
    return pltpu.CompilerParams(
        dimension_semantics=("arbitrary",) * n_grid,
        vmem_limit_bytes=int(min(vmem_bytes, VMEM_BUDGET)))


def _nbytes(shape, dtype):
    return math.prod(shape) * jnp.dtype(dtype).itemsize


def _rms(x, g):
    ms = jnp.mean(x * x, axis=-1, keepdims=True)
    return x * lax.rsqrt(ms + RMS_EPS) * g


def _norm_kernel(x_ref, g_ref, o_ref):
    o_ref[...] = _rms(x_ref[...], g_ref[...]).astype(o_ref.dtype)


def rms_norm_bf16(x, g, tm=512):
    t, d = x.shape
    return pl.pallas_call(
        _norm_kernel,
        grid=(t // tm,),
        in_specs=[pl.BlockSpec((tm, d), lambda i: (i, 0)),
                  pl.BlockSpec((1, d), lambda i: (0, 0))],
        out_specs=pl.BlockSpec((tm, d), lambda i: (i, 0)),
        out_shape=jax.ShapeDtypeStruct((t, d), BF16),
        compiler_params=_params(6 * tm * d * 4, 1),
        name="rms_norm",
    )(x, g.reshape(1, d))


def _gmm_kernel(te_ref, tr_ref, nx_ref, ti_ref, go_ref, x_ref, *refs, n_w, n_extra, n_out,
                col_offs, tn, sub, ragged, epilogue):
    del ti_ref
    w_refs = refs[:n_w]
    e_refs = refs[n_w:n_w + n_extra]
    o_refs = refs[n_w + n_extra:n_w + n_extra + n_out]
    stage, sem = refs[n_w + n_extra + n_out:]
    j = pl.program_id(0)
    i = pl.program_id(1)
    n_j = pl.num_programs(0)
    n_i = pl.num_programs(1)
    tm = x_ref.shape[0]
    cur = te_ref[i]
    first_of_group = jnp.logical_or(i == 0, cur != te_ref[jnp.maximum(i - 1, 0)])
    slot = (j * go_ref[n_i] + go_ref[i]) % 2

    def weight_copy(v, group, jj, to_slot):
        col = pl.multiple_of((jj + col_offs[v]) * tn, tn)
        return pltpu.make_async_copy(
            w_refs[v].at[group, :, pl.ds(col, tn)], stage.at[to_slot, v], sem.at[to_slot, v])

    @pl.when(jnp.logical_and(j == 0, i == 0))
    def _prime():
        for v in range(n_w):
            weight_copy(v, cur, j, slot).start()

    @pl.when(first_of_group)
    def _switch_weights():
        for v in range(n_w):
            weight_copy(v, cur, j, slot).wait()
        nxt = nx_ref[i]

        @pl.when(nxt >= 0)
        def _():
            for v in range(n_w):
                weight_copy(v, nxt, j, 1 - slot).start()

        @pl.when(jnp.logical_and(nxt < 0, j + 1 < n_j))
        def _():
            for v in range(n_w):
                weight_copy(v, te_ref[0], j + 1, 1 - slot).start()

    def compute(rows):
        xb = x_ref[rows, :]
        accs = [lax.dot_general(xb, stage[slot, v], (((1,), (0,)), ((), ())),
                                preferred_element_type=F32) for v in range(n_w)]
        epilogue(j, rows, accs, e_refs, o_refs)

    n_sub = tm // sub
    sub_rows = [slice(s * sub, (s + 1) * sub) for s in range(n_sub)]

    def tile_body(n_live):
        for s in range(n_live):
            compute(sub_rows[s])
        for s in range(n_live, n_sub):
            for o in o_refs:
                o[sub_rows[s], :] = jnp.zeros((sub, o.shape[1]), o.dtype)

    if not ragged:
        tile_body(n_sub)
    else:
        live = (tr_ref[i] + (sub - 1)) // sub
        for n_live in range(n_sub + 1):
            pl.when(live == n_live)(functools.partial(tile_body, n_live))


def grouped_matmul(x, weights, te, tr, *, tm, sub, tn, n_col_tiles, extras, out_dtypes,
                   out_cols, epilogue, name, ragged=False):
    rows, k = x.shape
    n_w, n_extra, n_out = len(weights), len(extras), len(out_dtypes)
    later = jnp.where(te[None, :] > te[:, None], te[None, :], jnp.iinfo(jnp.int32).max)
    nx = jnp.min(later, axis=1)
    nx = jnp.where(nx == jnp.iinfo(jnp.int32).max, -1, nx).astype(I32)
    tile_id = jnp.arange(rows // tm, dtype=I32)
    has_rows_before = jnp.logical_and(tile_id[None, :] <= tile_id[:, None], tr[None, :] > 0)
    ti = jnp.max(jnp.where(has_rows_before, tile_id[None, :], 0), axis=1).astype(I32)
    new_group = jnp.concatenate([jnp.zeros((1,), I32), (te[1:] != te[:-1]).astype(I32)])
    ordinal = jnp.sum(jnp.where(tile_id[None, :] <= tile_id[:, None], new_group[None, :], 0), axis=1)
    go = jnp.concatenate([ordinal, ordinal[-1:] + 1]).astype(I32)

    in_specs = [pl.BlockSpec((tm, k), lambda j, i, te, tr, nx, ti, go: (ti[i], 0))]
    in_specs += [pl.BlockSpec(memory_space=pl.ANY) for _ in weights]
    for _, bshape, imap in extras:
        in_specs.append(pl.BlockSpec(
            bshape, lambda j, i, te, tr, nx, ti, go, imap=imap: imap(j, ti[i])))
    out_specs = [pl.BlockSpec((tm, tn), lambda j, i, *_: (i, j)) for _ in out_dtypes]
    out_shape = [jax.ShapeDtypeStruct((rows, out_cols), dt) for dt in out_dtypes]

    vmem = 2 * _nbytes((tm, k), x.dtype)
    vmem += 2 * n_w * _nbytes((k, tn), F32)
    vmem += sum(2 * _nbytes(b, a.dtype) for a, b, _ in extras)
    vmem += sum(2 * _nbytes((tm, tn), dt) for dt in out_dtypes)
    vmem += (n_w + 3) * _nbytes((tm, tn), F32)
    vmem += _nbytes((tm, k), x.dtype) + 4 * 1024 * 1024

    return pl.pallas_call(
        functools.partial(_gmm_kernel, n_w=n_w, n_extra=n_extra, n_out=n_out,
                          col_offs=tuple(off for _, off in weights), tn=tn, sub=sub,
                          ragged=ragged, epilogue=epilogue),
        grid_spec=pltpu.PrefetchScalarGridSpec(
            num_scalar_prefetch=5,
            grid=(n_col_tiles, rows // tm),
            in_specs=in_specs,
            out_specs=out_specs,
            scratch_shapes=[pltpu.VMEM((2, n_w, k, tn), F32),
                            pltpu.SemaphoreType.DMA((2, n_w))]),
        out_shape=out_shape,
        compiler_params=_params(vmem, 2),
        name=name,
    )(te, tr, nx, ti, go, x, *[w for w, _ in weights], *[a for a, _, _ in extras])


def _dense_tiles(rows, tm, group):
    n = rows // tm
    return jnp.full((n,), group, I32), jnp.full((n,), tm, I32)


def _epi_swiglu(j, rows, accs, e_refs, o_refs):
    g, u = accs
    o_refs[0][rows, :] = (g * jax.nn.sigmoid(g) * u).astype(BF16)


def _epi_residual(j, rows, accs, e_refs, o_refs):
    o_refs[0][rows, :] = e_refs[0][rows, :] + accs[0]


def _epi_plain(j, rows, accs, e_refs, o_refs):
    o_refs[0][rows, :] = accs[0]


def _epi_conv_in(j, rows, accs, e_refs, o_refs):
    gate_b, gate_c, val = accs
    o_refs[0][rows, :] = gate_b.astype(BF16)
    o_refs[1][rows, :] = (gate_c * val).astype(BF16)


def _epi_qkv(j, rows, accs, e_refs, o_refs, *, heads_per_tile, n_rope_tiles):
    acc = accs[0]
    cos_ref, sin_ref, gain_ref = e_refs
    o = o_refs[0]
    cos = cos_ref[rows, :]
    sin = sin_ref[rows, :]
    is_qk = j < n_rope_tiles
    for hh in range(heads_per_tile):
        sl = slice(hh * HEAD_DIM, (hh + 1) * HEAD_DIM)
        a = acc[:, sl]
        y = _rms(a, gain_ref[:, sl])
        rot = pltpu.roll(y, HEAD_DIM // 2, axis=1)
        o[rows, sl] = jnp.where(is_qk, y * cos + rot * sin, a).astype(BF16)


def _conv_kernel(gb_ref, u_ref, up_ref, un_ref, w_ref, o_ref, *, tm, seq):
    i = pl.program_id(0)
    u = u_ref[...].astype(F32)
    w = w_ref[...]
    row = lax.broadcasted_iota(I32, u.shape, 0)
    starts_seq = (i * tm) % seq == 0
    ends_seq = ((i + 1) * tm) % seq == 0
    prev_row = up_ref[...].astype(F32)[BF16_SUBLANES - 1:BF16_SUBLANES, :]
    next_row = un_ref[...].astype(F32)[0:1, :]
    prev_row = jnp.where(starts_seq, 0.0, prev_row)
    next_row = jnp.where(ends_seq, 0.0, next_row)
    u_prev = jnp.where(row == 0, prev_row, pltpu.roll(u, 1, axis=0))
    u_next = jnp.where(row == tm - 1, next_row, pltpu.roll(u, tm - 1, axis=0))
    conv = w[0:1, :] * u_prev + w[1:2, :] * u + w[2:3, :] * u_next
    o_ref[...] = (gb_ref[...].astype(F32) * conv).astype(BF16)


def conv_gate(gb, u, w_conv, seq, tm=512):
    t, d = u.shape
    hb = tm // BF16_SUBLANES
    n_halo = t // BF16_SUBLANES
    return pl.pallas_call(
        functools.partial(_conv_kernel, tm=tm, seq=seq),
        grid=(t // tm,),
        in_specs=[pl.BlockSpec((tm, d), lambda i: (i, 0)),
                  pl.BlockSpec((tm, d), lambda i: (i, 0)),
                  pl.BlockSpec((BF16_SUBLANES, d), lambda i: (jnp.maximum(i * hb - 1, 0), 0)),
                  pl.BlockSpec((BF16_SUBLANES, d),
                               lambda i: (jnp.minimum((i + 1) * hb, n_halo - 1), 0)),
                  pl.BlockSpec((3, d), lambda i: (0, 0))],
        out_specs=pl.BlockSpec((tm, d), lambda i: (i, 0)),
        out_shape=jax.ShapeDtypeStruct((t, d), BF16),
        compiler_params=_params(12 * tm * d * 4, 1),
        name="conv_gate",
    )(gb, u, u, u, w_conv)


def _rope_kernel(pos_ref, invf_ref, sign_ref, cos_ref, sin_ref):
    ang = pos_ref[...].astype(F32) * invf_ref[...]
    cos_ref[...] = jnp.cos(ang)
    sin_ref[...] = jnp.sin(ang) * sign_ref[...]


def rope_tables(positions, tm=512):
    t = positions.shape[0]
    half = HEAD_DIM // 2
    inv_freq = ROPE_THETA ** (-jnp.arange(0, half, dtype=F32) / half)
    invf = jnp.concatenate([inv_freq, inv_freq]).reshape(1, HEAD_DIM)
    sign = jnp.concatenate([-jnp.ones((half,), F32), jnp.ones((half,), F32)]).reshape(1, HEAD_DIM)
    return pl.pallas_call(
        _rope_kernel,
        grid=(t // tm,),
        in_specs=[pl.BlockSpec((tm, 1), lambda i: (i, 0)),
                  pl.BlockSpec((1, HEAD_DIM), lambda i: (0, 0)),
                  pl.BlockSpec((1, HEAD_DIM), lambda i: (0, 0))],
        out_specs=[pl.BlockSpec((tm, HEAD_DIM), lambda i: (i, 0))] * 2,
        out_shape=[jax.ShapeDtypeStruct((t, HEAD_DIM), F32)] * 2,
        compiler_params=_params(16 * tm * HEAD_DIM * 4, 1),
        name="rope_tables",
    )(positions.reshape(t, 1), invf, sign)


ATTN_ROWS = Q_PER_KV * ATTN_BLOCK
ATTN_KEYS = 3 * ATTN_BLOCK
ATTN_UNROLL = 4


def _attn_kernel(q_ref, k_ref, v_ref, sink_ref, o_ref, bias_ref):
    seq = q_ref.shape[0]
    sink = jnp.broadcast_to(sink_ref[...], (ATTN_ROWS, HEAD_DIM))
    q_off = lax.broadcasted_iota(I32, (ATTN_BLOCK, ATTN_KEYS), 0)
    k_off = lax.broadcasted_iota(I32, (ATTN_BLOCK, ATTN_KEYS), 1)
    for c in range(3):
        inside = jnp.abs(k_off - q_off - c * ATTN_BLOCK) <= WINDOW
        bias_ref[c] = jnp.where(inside, 0.0, MASK_VALUE)
    ones = jnp.ones((ATTN_KEYS, HEAD_DIM), BF16)

    def block(n):
        q0 = pl.multiple_of(n * ATTN_BLOCK, ATTN_BLOCK)
        k0 = pl.multiple_of(jnp.clip((n - 1) * ATTN_BLOCK, 0, seq - ATTN_KEYS), ATTN_BLOCK)
        qb = q_ref[pl.ds(q0, ATTN_BLOCK), :]
        qs = jnp.concatenate(
            [qb[:, g * HEAD_DIM:(g + 1) * HEAD_DIM] for g in range(Q_PER_KV)], axis=0)
        kw = k_ref[pl.ds(k0, ATTN_KEYS), :]
        v_ext = jnp.concatenate([v_ref[pl.ds(k0, ATTN_KEYS), :], ones], axis=1)
        s_all = lax.dot_general(qs, kw, (((1,), (1,)), ((), ())), preferred_element_type=F32)
        bias = bias_ref[(q0 - k0) // ATTN_BLOCK]
        ms, ps = [], []
        for g in range(Q_PER_KV):
            rows = slice(g * ATTN_BLOCK, (g + 1) * ATTN_BLOCK)
            s = s_all[rows, :] + bias
            m = jnp.maximum(jnp.max(s, axis=-1, keepdims=True), sink[rows, :])
            p = jnp.exp(s - jnp.concatenate([m] * (ATTN_KEYS // HEAD_DIM), axis=1))
            ms.append(m)
            ps.append(p.astype(BF16))
        pv = jnp.dot(jnp.concatenate(ps, axis=0), v_ext, preferred_element_type=F32)
        for g in range(Q_PER_KV):
            rows = slice(g * ATTN_BLOCK, (g + 1) * ATTN_BLOCK)
            denom = pv[rows, HEAD_DIM:] + jnp.exp(sink[rows, :] - ms[g])
            o = pv[rows, :HEAD_DIM] / denom
            o_ref[pl.ds(q0, ATTN_BLOCK), g * HEAD_DIM:(g + 1) * HEAD_DIM] = o.astype(BF16)

    def body(n2, carry):
        for r in range(ATTN_UNROLL):
            block(n2 * ATTN_UNROLL + r)
        return carry

    lax.fori_loop(0, seq // (ATTN_BLOCK * ATTN_UNROLL), body, 0)


def window_attention(qkv, sink, batch, seq):
    t = qkv.shape[0]
    n_heads = N_KV_HEADS * Q_PER_KV
    gw = Q_PER_KV * HEAD_DIM
    sink_col = jnp.repeat(sink.astype(F32).reshape(N_KV_HEADS, Q_PER_KV), ATTN_BLOCK, axis=1)
    sink_col = sink_col.reshape(N_KV_HEADS, ATTN_ROWS, 1)
    return pl.pallas_call(
        _attn_kernel,
        grid=(batch, N_KV_HEADS),
        in_specs=[pl.BlockSpec((seq, gw), lambda b, h: (b, h)),
                  pl.BlockSpec((seq, HEAD_DIM), lambda b, h: (b, n_heads + h)),
                  pl.BlockSpec((seq, HEAD_DIM), lambda b, h: (b, n_heads + N_KV_HEADS + h)),
                  pl.BlockSpec((None, ATTN_ROWS, 1), lambda b, h: (h, 0, 0))],
        out_specs=pl.BlockSpec((seq, gw), lambda b, h: (b, h)),
        out_shape=jax.ShapeDtypeStruct((t, n_heads * HEAD_DIM), BF16),
        scratch_shapes=[pltpu.VMEM((3, ATTN_BLOCK, ATTN_KEYS), F32)],
        compiler_params=_params(6 * seq * gw * 2 + 24 * 1024 * 1024, 2),
        name="window_attention",
    )(qkv, qkv, qkv, sink_col)


def _route_kernel(x_ref, g_ref, r_ref, h_ref, meta_ref, gate_ref, cnt_ref, carry_ref, *, tm):
    i = pl.program_id(0)

    @pl.when(i == 0)
    def _init():
        carry_ref[...] = jnp.zeros(carry_ref.shape, carry_ref.dtype)

    h = _rms(x_ref[...], g_ref[...])
    h_ref[...] = h.astype(BF16)
    r = r_ref[...]
    h_hi = h.astype(BF16)
    h_lo = (h - h_hi.astype(F32)).astype(BF16)
    r_hi = r.astype(BF16)
    r_lo = (r - r_hi.astype(F32)).astype(BF16)
    logits = (jnp.dot(h_hi, r_hi, preferred_element_type=F32)
              + (jnp.dot(h_hi, r_lo, preferred_element_type=F32)
                 + jnp.dot(h_lo, r_hi, preferred_element_type=F32)))
    lane = lax.broadcasted_iota(I32, logits.shape, 1)
    lane_f = lane.astype(F32)
    neg_inf = jnp.float32(-jnp.inf)
    logits = jnp.where(lane < N_EXPERTS, logits, neg_inf)
    v1 = jnp.max(logits, axis=-1, keepdims=True)
    i1 = jnp.min(jnp.where(logits == v1, lane_f, float(LANES)), axis=-1, keepdims=True)
    rest = jnp.where(lane_f == i1, neg_inf, logits)
    v2 = jnp.max(rest, axis=-1, keepdims=True)
    i2 = jnp.min(jnp.where(rest == v2, lane_f, float(LANES)), axis=-1, keepdims=True)
    e21 = jnp.exp(v2 - v1)
    g1 = 1.0 / (1.0 + e21)
    g2 = e21 / (1.0 + e21)

    sel1 = lane_f == i1
    sel2 = lane_f == i2
    onehot = jnp.logical_or(sel1, sel2)
    r_io = lax.broadcasted_iota(I32, (tm, tm), 0)
    c_io = lax.broadcasted_iota(I32, (tm, tm), 1)
    strict_lower = (c_io < r_io).astype(BF16)
    carry = carry_ref[...]
    before = jnp.dot(strict_lower, onehot.astype(BF16), preferred_element_type=F32) + carry
    rank1 = jnp.sum(jnp.where(sel1, before, 0.0), axis=-1, keepdims=True)
    rank2 = jnp.sum(jnp.where(sel2, before, 0.0), axis=-1, keepdims=True)
    carry = carry + jnp.sum(onehot.astype(F32), axis=0, keepdims=True)
    carry_ref[...] = carry
    cnt_ref[...] = carry.astype(I32)

    meta = jnp.where(lane == 0, i1, jnp.where(lane == 1, i2,
           jnp.where(lane == 2, rank1, jnp.where(lane == 3, rank2, 0.0))))
    meta_ref[...] = meta.astype(I32)
    gate_ref[...] = jnp.where(lane == 0, g1, jnp.where(lane == 1, g2, 0.0))


def moe_route(x, g, w_router, tm=512):
    t, d = x.shape
    r_pad = jnp.zeros((d, LANES), F32).at[:, :N_EXPERTS].set(w_router)
    return pl.pallas_call(
        functools.partial(_route_kernel, tm=tm),
        grid=(t // tm,),
        in_specs=[pl.BlockSpec((tm, d), lambda i: (i, 0)),
                  pl.BlockSpec((1, d), lambda i: (0, 0)),
                  pl.BlockSpec((d, LANES), lambda i: (0, 0))],
        out_specs=[pl.BlockSpec((tm, d), lambda i: (i, 0)),
                   pl.BlockSpec((tm, LANES), lambda i: (i, 0)),
                   pl.BlockSpec((tm, LANES), lambda i: (i, 0)),
                   pl.BlockSpec((1, LANES), lambda i: (0, 0))],
        out_shape=[jax.ShapeDtypeStruct((t, d), BF16),
                   jax.ShapeDtypeStruct((t, LANES), I32),
                   jax.ShapeDtypeStruct((t, LANES), F32),
                   jax.ShapeDtypeStruct((1, LANES), I32)],
        scratch_shapes=[pltpu.VMEM((1, LANES), F32)],
        compiler_params=_params(10 * tm * d * 4 + 8 * 1024 * 1024, 1),
        name="moe_route",
    )(x, g.reshape(1, d), r_pad)


DISPATCH_TM = 512


def _dispatch_kernel(pos_ref, h_ref, zeros_hbm, xs_hbm, sem):
    del zeros_hbm
    tm = h_ref.shape[0]
    base = 2 * pl.program_id(0) * tm
    for r in range(tm):
        for k in range(2):
            pltpu.make_async_copy(
                h_ref.at[r], xs_hbm.at[pos_ref[base + 2 * r + k]], sem).start(priority=k)
    for k in range(2):
        pltpu.make_async_copy(h_ref, xs_hbm.at[pl.ds(0, tm)], sem).wait()


def moe_dispatch(h, pos_flat, n_slots):
    t, d = h.shape
    h3 = h.reshape(t, d // LANES, LANES)
    zeros = jnp.zeros((n_slots, d // LANES, LANES), BF16)
    tm = DISPATCH_TM
    xs = pl.pallas_call(
        _dispatch_kernel,
        grid_spec=pltpu.PrefetchScalarGridSpec(
            num_scalar_prefetch=1,
            grid=(t // tm,),
            in_specs=[pl.BlockSpec((tm, d // LANES, LANES), lambda i, pos: (i, 0, 0)),
                      pl.BlockSpec(memory_space=pl.ANY)],
            out_specs=pl.BlockSpec(memory_space=pl.ANY),
            scratch_shapes=[pltpu.SemaphoreType.DMA(())]),
        out_shape=jax.ShapeDtypeStruct(zeros.shape, BF16),
        input_output_aliases={2: 0},
        compiler_params=_params(4 * tm * d * 2 + 4 * 1024 * 1024, 1),
        name="moe_dispatch",
    )(pos_flat, h3, zeros)
    return xs.reshape(n_slots, d)


def _combine_kernel(pos_ref, x_ref, gate_ref, y_hbm, g_ref, *refs, tc, with_norm):
    if with_norm:
        xo_ref, ho_ref, buf, sem = refs
    else:
        xo_ref, buf, sem = refs
    i = pl.program_id(0)

    def issue(tile, slot):
        base = 2 * tile * tc
        for r in range(tc):
            for k in range(2):
                pltpu.make_async_copy(
                    y_hbm.at[pl.ds(pos_ref[base + 2 * r + k], 1), :],
                    buf.at[slot, pl.ds(k * tc + r, 1), :], sem.at[slot]).start(priority=k)

    @pl.when(i == 0)
    def _prime():
        issue(0, 0)

    for s in range(2):
        @pl.when(jnp.logical_and(i + 1 < pl.num_programs(0), (i + 1) % 2 == s))
        def _prefetch():
            issue(i + 1, s)

    slot = i % 2
    pltpu.make_async_copy(y_hbm.at[pl.ds(0, 2 * tc), :], buf.at[slot], sem.at[slot]).wait()
    gates = gate_ref[...]
    y = gates[:, 0:1] * buf[slot, 0:tc, :] + gates[:, 1:2] * buf[slot, tc:2 * tc, :]
    xn = x_ref[...] + y
    xo_ref[...] = xn
    if with_norm:
        ho_ref[...] = _rms(xn, g_ref[...]).astype(BF16)


def moe_combine(x, gates, y, pos_flat, g_next, tc=256):
    t, d = x.shape
    with_norm = g_next is not None
    g_arr = (g_next if with_norm else jnp.ones((d,), F32)).reshape(1, d)
    out_specs = [pl.BlockSpec((tc, d), lambda i, pos: (i, 0))]
    out_shape = [jax.ShapeDtypeStruct((t, d), F32)]
    if with_norm:
        out_specs.append(pl.BlockSpec((tc, d), lambda i, pos: (i, 0)))
        out_shape.append(jax.ShapeDtypeStruct((t, d), BF16))
    outs = pl.pallas_call(
        functools.partial(_combine_kernel, tc=tc, with_norm=with_norm),
        grid_spec=pltpu.PrefetchScalarGridSpec(
            num_scalar_prefetch=1,
            grid=(t // tc,),
            in_specs=[pl.BlockSpec((tc, d), lambda i, pos: (i, 0)),
                      pl.BlockSpec((tc, LANES), lambda i, pos: (i, 0)),
                      pl.BlockSpec(memory_space=pl.ANY),
                      pl.BlockSpec((1, d), lambda i, pos: (0, 0))],
            out_specs=out_specs,
            scratch_shapes=[pltpu.VMEM((2, 2 * tc, d), F32), pltpu.SemaphoreType.DMA((2,))]),
        out_shape=out_shape,
        compiler_params=_params(20 * tc * d * 4 + 4 * 1024 * 1024, 1),
        name="moe_combine",
    )(pos_flat, x, gates, y, g_arr)
    return outs if with_norm else (outs[0], None)


def moe_layer(x, g_ffn, w_router, w13, w2, layer, g_next):
    t, d = x.shape
    d_ff = w2.shape[1]
    tm = MOE_TM
    n_tiles = (2 * t) // tm + N_EXPERTS
    n_slots = n_tiles * tm

    h, meta, gates, counts = moe_route(x, g_ffn, w_router)
    cnt = counts[0, :N_EXPERTS]
    tiles_per = (cnt + tm - 1) // tm
    tile_end = jnp.cumsum(tiles_per)
    tile_start = tile_end - tiles_per
    row_start = tile_start * tm
    e1, e2, r1, r2 = meta[:, 0], meta[:, 1], meta[:, 2], meta[:, 3]
    pos = jnp.stack([row_start[e1] + r1, row_start[e2] + r2], axis=1).reshape(-1).astype(I32)
    tile_id = jnp.arange(n_tiles, dtype=I32)
    used = tile_end[-1]
    te = jnp.sum(jnp.minimum(tile_id, used - 1)[:, None] >= tile_end[None, :], axis=1).astype(I32)
    tr = jnp.clip(cnt[te] - (tile_id - tile_start[te]) * tm, 0, tm)
    tr = jnp.where(tile_id < used, tr, 0).astype(I32)
    te = te + layer * N_EXPERTS

    xs = moe_dispatch(h, pos, n_slots)

    tn13 = 1024
    (act,) = grouped_matmul(
        xs, [(w13, 0), (w13, d_ff // tn13)], te, tr, tm=tm, sub=MOE_SUB_W13, tn=tn13,
        n_col_tiles=d_ff // tn13, extras=[], out_dtypes=[BF16], out_cols=d_ff,
        epilogue=_epi_swiglu, name="moe_w13", ragged=True)
    tn2 = 512
    tm2 = MOE_TM_W2
    parts = tm // tm2
    te2 = jnp.repeat(te, parts)
    tr2 = jnp.clip(tr[:, None] - jnp.arange(parts, dtype=I32)[None, :] * tm2, 0, tm2).reshape(-1)
    (y,) = grouped_matmul(
        act, [(w2, 0)], te2, tr2, tm=tm2, sub=MOE_SUB_W2, tn=tn2, n_col_tiles=d // tn2, extras=[],
        out_dtypes=[F32], out_cols=d, epilogue=_epi_plain, name="moe_w2", ragged=True)
    return moe_combine(x, gates, y, pos, g_next)


def conv_layer(x, h, w_in, w_conv, w_out, layer, seq):
    t, d = x.shape
    tn = 512
    tm = 1024
    te, tr = _dense_tiles(t, tm, layer)
    nt = d // tn
    gb, u = grouped_matmul(
        h, [(w_in, 0), (w_in, nt), (w_in, 2 * nt)], te, tr, tm=tm, sub=tm // 2, tn=tn,
        n_col_tiles=nt, extras=[], out_dtypes=[BF16, BF16], out_cols=d,
        epilogue=_epi_conv_in, name="conv_in")
    z = conv_gate(gb, u, w_conv, seq)
    tn = 1024
    (x,) = grouped_matmul(
        z, [(w_out, 0)], te, tr, tm=tm, sub=tm // 2, tn=tn, n_col_tiles=d // tn,
        extras=[(x, (tm, tn), lambda j, i: (i, j))], out_dtypes=[F32], out_cols=d,
        epilogue=_epi_residual, name="conv_out")
    return x


def dense_ffn(x, h, w13, w2, layer):
    t, d = x.shape
    d_ff = w2.shape[1]
    tn = 1024
    tm = 1024
    te, tr = _dense_tiles(t, tm, layer)
    (act,) = grouped_matmul(
        h, [(w13, 0), (w13, d_ff // tn)], te, tr, tm=tm, sub=512, tn=tn,
        n_col_tiles=d_ff // tn, extras=[], out_dtypes=[BF16], out_cols=d_ff,
        epilogue=_epi_swiglu, name="ffn_w13")
    tn = 512
    tm = 512
    te, tr = _dense_tiles(t, tm, layer)
    (x,) = grouped_matmul(
        act, [(w2, 0)], te, tr, tm=tm, sub=tm // 2, tn=tn, n_col_tiles=d // tn,
        extras=[(x, (tm, tn), lambda j, i: (i, j))], out_dtypes=[F32], out_cols=d,
        epilogue=_epi_residual, name="ffn_w2")
    return x


def attention_layer(x, h, cos, sin, w_qkv, q_gain, k_gain, sink, w_out, layer, batch, seq):
    t, d = x.shape
    n_heads = N_KV_HEADS * Q_PER_KV
    tn = 512
    tm = 1024
    heads_per_tile = tn // HEAD_DIM
    qkv_dim = w_qkv.shape[2]
    n_rope_tiles = (n_heads + N_KV_HEADS) // heads_per_tile
    gain_cols = jnp.concatenate([
        jnp.tile(q_gain.astype(F32) * (1.0 / math.sqrt(HEAD_DIM)), n_heads),
        jnp.tile(k_gain.astype(F32), N_KV_HEADS),
        jnp.ones((N_KV_HEADS * HEAD_DIM,), F32)]).reshape(1, qkv_dim)
    te, tr = _dense_tiles(t, tm, layer)
    (qkv,) = grouped_matmul(
        h, [(w_qkv, 0)], te, tr, tm=tm, sub=256, tn=tn, n_col_tiles=qkv_dim // tn,
        extras=[(cos, (tm, HEAD_DIM), lambda j, i: (i, 0)),
                (sin, (tm, HEAD_DIM), lambda j, i: (i, 0)),
                (gain_cols, (1, tn), lambda j, i: (0, j))],
        out_dtypes=[BF16], out_cols=qkv_dim,
        epilogue=functools.partial(_epi_qkv, heads_per_tile=heads_per_tile,
                                   n_rope_tiles=n_rope_tiles),
        name="attn_qkv")
    o = window_attention(qkv, sink, batch, seq)
    tn = 1024
    (x,) = grouped_matmul(
        o, [(w_out, 0)], te, tr, tm=tm, sub=tm // 2, tn=tn, n_col_tiles=d // tn,
        extras=[(x, (tm, tn), lambda j, i: (i, j))], out_dtypes=[F32], out_cols=d,
        epilogue=_epi_residual, name="attn_out")
    return x


def kernel(x, positions, norm_mix, norm_ffn, conv_in, conv_w, conv_out, attn_qkv, q_norm, k_norm,
           attn_sink, attn_out, ffn_w13, ffn_w2, router, moe_w13, moe_w2):
    batch, seq, d = x.shape
    depth = norm_mix.shape[0]
    t = batch * seq
    x = x.reshape(t, d)
    moe_w13 = moe_w13.reshape((-1,) + moe_w13.shape[2:])
    moe_w2 = moe_w2.reshape((-1,) + moe_w2.shape[2:])
    cos, sin = rope_tables(positions.reshape(t))
    h = rms_norm_bf16(x, norm_mix[0])
    for i in range(depth):
        j = i // 2
        if i % 2 == 0:
            x = conv_layer(x, h, conv_in, conv_w[j], conv_out, j, seq)
            h = rms_norm_bf16(x, norm_ffn[i])
            x = dense_ffn(x, h, ffn_w13, ffn_w2, j)
            h = rms_norm_bf16(x, norm_mix[i + 1])
        else:
            x = attention_layer(x, h, cos, sin, attn_qkv, q_norm[j], k_norm[j],
                                attn_sink[j], attn_out, j, batch, seq)
            g_next = norm_mix[i + 1] if i + 1 < depth else None
            x, h = moe_layer(x, norm_ffn[i], router[j], moe_w13, moe_w2, j, g_next)
    return x.reshape(batch, seq, d)
```
